```python
import math
import numpy as np
import jax
import jax.numpy as jnp
from jax import lax

D_MODEL = 1024
BATCH = 32
SEQ = 256
DEPTH = 4
DEC_BATCH = 2
DEC_SEQ = 4096
PAST_LEN = 512

GRID_W = 64
DA_HEADS = 4
DA_QK = 64
DA_V = 2 * DA_QK
ML_HEADS = 4
ML_DIM = 128
ML_CHUNK = 64
ML_FORGET_BIAS = 3.0
NA_HEADS = 8
NA_DIM = 64
NA_WIN_ROWS = 8
NA_WIN_COLS = 16
N_BRANCH = 3
D_FF = 4 * D_MODEL
Q_BLOCK = 128
ROPE_BASE = 10000.0
EPS = 1e-6
DA_W = DA_HEADS * DA_V
ML_W = ML_HEADS * ML_DIM
NA_W = NA_HEADS * NA_DIM
SPLIT_SIZES = (
    DA_HEADS * 2 * DA_QK,
    DA_HEADS * 2 * DA_QK,
    DA_W,
    ML_W,
    ML_W,
    ML_W,
    ML_W,
    2 * 2 * ML_HEADS,
    NA_W,
    NA_W,
    NA_W,
    N_BRANCH * D_MODEL,
)
N_PROJ = sum(SPLIT_SIZES)
ML_GATE_OFF = sum(SPLIT_SIZES[:7])

kernel_name = 'hybrid_diffattn_mlstm_natten_prefix_step'


def rmsnorm(x, g):
    xf = x.astype(jnp.float32)
    y = xf * lax.rsqrt(jnp.mean(xf * xf, axis=-1, keepdims=True) + EPS)
    return (y * g.astype(jnp.float32)).astype(x.dtype)


def rope_1d(x, pos):
    d = x.shape[-1]
    freqs = ROPE_BASE ** (-jnp.arange(0, d, 2, dtype=jnp.float32) / d)
    ang = pos[:, None] * freqs[None, :]
    cos = jnp.cos(ang)[None, :, None, :]
    sin = jnp.sin(ang)[None, :, None, :]
    xf = x.astype(jnp.float32)
    x1, x2 = xf[..., : d // 2], xf[..., d // 2:]
    return jnp.concatenate([x1 * cos - x2 * sin, x1 * sin + x2 * cos], axis=-1).astype(x.dtype)


def rope_2d(x, rows, cols):
    half = x.shape[-1] // 2
    return jnp.concatenate([rope_1d(x[..., :half], rows), rope_1d(x[..., half:], cols)], axis=-1)


def over_query_blocks(fn, *qs):
    b, t = qs[0].shape[:2]
    nb = t // Q_BLOCK
    blocks = tuple(jnp.moveaxis(q.reshape((b, nb, Q_BLOCK) + q.shape[2:]), 1, 0) for q in qs)
    out = lax.map(lambda blk: fn(*blk), blocks)
    out = jnp.moveaxis(out, 0, 1)
    return out.reshape((b, t) + out.shape[3:])


def softmax_attention(q, k, v):
    scale = q.shape[-1] ** -0.5

    def block(qb):
        s = jnp.einsum('bqhd,bkhd->bhqk', qb, k).astype(jnp.float32) * scale
        p = jax.nn.softmax(s, axis=-1)
        return jnp.einsum('bhqk,bkhe->bqhe', p.astype(v.dtype), v)

    return over_query_blocks(block, q)


def diff_attention(q1, q2, k1, k2, v, lam):
    scale = q1.shape[-1] ** -0.5

    def block(q1b, q2b):
        p1 = jax.nn.softmax(jnp.einsum('bqhd,bkhd->bhqk', q1b, k1).astype(jnp.float32) * scale, axis=-1)
        p2 = jax.nn.softmax(jnp.einsum('bqhd,bkhd->bhqk', q2b, k2).astype(jnp.float32) * scale, axis=-1)
        p = p1 - lam * p2
        return jnp.einsum('bhqk,bkhe->bqhe', p.astype(v.dtype), v)

    return over_query_blocks(block, q1, q2)


def neighbourhood_attention(q, k, v, k_ctx, v_ctx, rpb):
    b, t, h, d = q.shape
    rows = t // GRID_W
    wr = min(NA_WIN_ROWS, rows)
    scale = d ** -0.5
    qg = q.reshape(b, rows, GRID_W, h, d)
    kg = k.reshape(b, rows, GRID_W, h, d)
    vg = v.reshape(b, rows, GRID_W, h, d)
    r = jnp.arange(rows)
    row_idx = jnp.clip(r - wr // 2, 0, rows - wr)[:, None] + jnp.arange(wr)[None, :]
    k_rows = kg[:, row_idx]
    v_rows = vg[:, row_idx]
    s_loc = jnp.einsum('brchd,brijhd->brhcij', qg, k_rows).astype(jnp.float32) * scale
    c = jnp.arange(GRID_W)
    col_start = jnp.clip(c - NA_WIN_COLS // 2, 0, GRID_W - NA_WIN_COLS)
    col_ok = (c[None, :] >= col_start[:, None]) & (c[None, :] < col_start[:, None] + NA_WIN_COLS)
    dr = row_idx - r[:, None] + (NA_WIN_ROWS - 1)
    dc = jnp.clip(c[None, :] - c[:, None] + (NA_WIN_COLS - 1), 0, 2 * NA_WIN_COLS - 2)
    bias = rpb[:, dr[:, None, :, None], dc[None, :, None, :]]
    bias = jnp.moveaxis(bias, 0, 1).astype(jnp.float32)
    s_loc = jnp.where(col_ok[:, None, :], s_loc + bias, -jnp.inf)
    s_ctx = jnp.einsum('brchd,bphd->brhcp', qg, k_ctx).astype(jnp.float32) * scale
    n_loc = wr * GRID_W
    s = jnp.concatenate([s_loc.reshape(b, rows, h, GRID_W, n_loc), s_ctx], axis=-1)
    p = jax.nn.softmax(s, axis=-1)
    p_loc = p[..., :n_loc].reshape(b, rows, h, GRID_W, wr, GRID_W).astype(v.dtype)
    p_ctx = p[..., n_loc:].astype(v.dtype)
    out = (jnp.einsum('brhcij,brijhd->brchd', p_loc, v_rows)
           + jnp.einsum('brhcp,bphd->brchd', p_ctx, v_ctx))
    return out.reshape(b, t, h, d)


def mlstm_chunked(q, k, v, ig, lf, c0, n0, m0):
    b, t, h, d = q.shape
    f32 = jnp.float32
    L = ML_CHUNK
    nc = t // L
    k = k.astype(f32) * (d ** -0.5)

    def to_chunks(a):
        a = a.astype(f32).reshape((b, nc, L) + a.shape[2:])
        return jnp.moveaxis(jnp.moveaxis(a, 1, 0), 2, 3)

    causal = jnp.tril(jnp.ones((L, L), dtype=bool))

    def step(carry, inp):
        cm, nm, mm = carry
        qc, kc, vc, ic, fc = inp
        bcum = jnp.cumsum(fc, axis=-1)
        logd = bcum[..., :, None] - bcum[..., None, :] + ic[..., None, :]
        logd = jnp.where(causal, logd, -jnp.inf)
        m_t = jnp.maximum(bcum + mm[..., None], jnp.max(logd, axis=-1))
        inter = jnp.exp(bcum + mm[..., None] - m_t)
        sc = jnp.einsum('bhtd,bhsd->bhts', qc, kc) * jnp.exp(logd - m_t[..., None])
        num = inter[..., None] * jnp.einsum('bhtd,bhde->bhte', qc, cm) + jnp.einsum('bhts,bhse->bhte', sc, vc)
        den = inter * jnp.einsum('bhtd,bhd->bht', qc, nm) + jnp.sum(sc, axis=-1)
        hc = num / jnp.maximum(jnp.abs(den), jnp.exp(-m_t))[..., None]
        m_new = m_t[..., -1]
        w = jnp.exp(bcum[..., -1:] - bcum + ic - m_new[..., None])
        decay = jnp.exp(bcum[..., -1] + mm - m_new)
        c_new = decay[..., None, None] * cm + jnp.einsum('bhs,bhsd,bhse->bhde', w, kc, vc)
        n_new = decay[..., None] * nm + jnp.einsum('bhs,bhsd->bhd', w, kc)
        return (c_new, n_new, m_new), hc

    init = (c0.astype(f32), n0.astype(f32), m0.astype(f32))
    final, hs = lax.scan(step, init, tuple(to_chunks(a) for a in (q, k, v, ig, lf)))
    hs = jnp.moveaxis(jnp.moveaxis(hs, 3, 2), 0, 1).reshape(b, t, h, d)
    return hs, final


def mlstm_bidir(q, k, v, gates, c0, n0, m0):
    hs, cs, ns, ms = [], [], [], []
    for dr in range(2):
        ig = gates[:, :, dr, 0]
        lf = jax.nn.log_sigmoid(gates[:, :, dr, 1].astype(jnp.float32))
        seq = (q, k, v, ig, lf)
        if dr == 1:
            seq = tuple(jnp.flip(a, axis=1) for a in seq)
        hd, (cf, nf, mf) = mlstm_chunked(*seq, c0[:, dr], n0[:, dr], m0[:, dr])
        if dr == 1:
            hd = jnp.flip(hd, axis=1)
        hs.append(hd)
        cs.append(cf)
        ns.append(nf)
        ms.append(mf)
    return hs[0] + hs[1], jnp.stack(cs, axis=1), jnp.stack(ns, axis=1), jnp.stack(ms, axis=1)


def layer(x, mod, lp, lam_init, ctx):
    b, t, _ = x.shape
    f32 = jnp.float32
    is_context = ctx is None
    sh1, sc1, g1, sh2, sc2, g2 = jnp.split(mod, 6, axis=-1)
    h = rmsnorm(x, lp['norm1']) * (1 + sc1) + sh1
    proj = h @ lp['w_in'] + lp['b_in']
    (da_q, da_k, da_v, ml_q, ml_k, ml_v, ml_o, ml_g,
     na_q, na_k, na_v, merge) = jnp.split(proj, np.cumsum(SPLIT_SIZES)[:-1].tolist(), axis=-1)

    dq = da_q.reshape(b, t, 2 * DA_HEADS, DA_QK)
    dk = da_k.reshape(b, t, 2 * DA_HEADS, DA_QK)
    dv = da_v.reshape(b, t, DA_HEADS, DA_V)
    if is_context:
        k_all = dk.reshape(b, t, DA_HEADS, 2 * DA_QK)
        v_all = dv
    else:
        pos = jnp.arange(t)
        rows = (pos // GRID_W).astype(f32)
        cols = (pos % GRID_W).astype(f32)
        dq = rope_2d(dq, rows, cols)
        dk = rope_2d(dk, rows, cols)
        k_all = jnp.concatenate([dk.reshape(b, t, DA_HEADS, 2 * DA_QK), ctx[0]], axis=1)
        v_all = jnp.concatenate([dv, ctx[1]], axis=1)
    q4 = dq.reshape(b, t, DA_HEADS, 2, DA_QK)
    k4 = k_all.reshape(b, -1, DA_HEADS, 2, DA_QK)
    lv = lp['da_lam'].astype(f32)
    lam = jnp.exp(jnp.sum(lv[0] * lv[1])) - jnp.exp(jnp.sum(lv[2] * lv[3])) + lam_init
    o_da = diff_attention(q4[..., 0, :], q4[..., 1, :], k4[..., 0, :], k4[..., 1, :], v_all, lam)
    o_da = (rmsnorm(o_da, lp['da_subln']) * (1.0 - lam_init)).reshape(b, t, DA_W)

    mq = ml_q.reshape(b, t, ML_HEADS, ML_DIM)
    mk = ml_k.reshape(b, t, ML_HEADS, ML_DIM)
    mv = ml_v.reshape(b, t, ML_HEADS, ML_DIM)
    gates = ml_g.reshape(b, t, 2, 2, ML_HEADS)
    if is_context:
        c0 = jnp.zeros((b, 2, ML_HEADS, ML_DIM, ML_DIM), f32)
        n0 = jnp.zeros((b, 2, ML_HEADS, ML_DIM), f32)
        m0 = jnp.zeros((b, 2, ML_HEADS), f32)
    else:
        c0, n0, m0 = ctx[4], ctx[5], ctx[6]
    h_ml, c_f, n_f, m_f = mlstm_bidir(mq, mk, mv, gates, c0, n0, m0)
    o_ml = (rmsnorm(h_ml, lp['ml_norm'].reshape(ML_HEADS, ML_DIM)).reshape(b, t, ML_W)
            * jax.nn.sigmoid(ml_o.astype(f32))).astype(x.dtype)

    nq = na_q.reshape(b, t, NA_HEADS, NA_DIM)
    nk = na_k.reshape(b, t, NA_HEADS, NA_DIM)
    nv = na_v.reshape(b, t, NA_HEADS, NA_DIM)
    if is_context:
        o_na = softmax_attention(nq, nk, nv)
    else:
        o_na = neighbourhood_attention(nq, nk, nv, ctx[2], ctx[3], lp['na_rpb'])
    o_na = o_na.reshape(b, t, NA_W)

    gate = jax.nn.sigmoid(merge.astype(f32)).astype(x.dtype).reshape(b, t, N_BRANCH, D_MODEL)
    merged = (gate[:, :, 0] * (o_da @ lp['w_up_da'])
              + gate[:, :, 1] * (o_ml @ lp['w_up_ml'])
              + gate[:, :, 2] * (o_na @ lp['w_up_na']))
    x = x + g1 * (merged @ lp['w_out'])

    h2 = rmsnorm(x, lp['norm2']) * (1 + sc2) + sh2
    u = jnp.square(jax.nn.relu(h2 @ lp['w_ff1'] + lp['b_ff1']))
    x = x + g2 * (u @ lp['w_ff2'] + lp['b_ff2'])
    new_ctx = (k_all, v_all, nk, nv, c_f, n_f, m_f) if is_context else None
    return x, new_ctx


def setup_inputs(seed: int = 0) -> dict:
    key = jax.random.key(seed)
    ks = jax.random.split(key, 40)
    f32 = jnp.float32

    def nrm(k, shape, s):
        return s * jax.random.normal(k, shape, f32)

    f_cols = ML_GATE_OFF + np.array([dr * 2 * ML_HEADS + ML_HEADS + hh for dr in range(2) for hh in range(ML_HEADS)])
    b_in = nrm(ks[15], (DEPTH, N_PROJ), 0.02).at[:, f_cols].add(ML_FORGET_BIAS)
    return {
        'x_prompt': nrm(ks[0], (BATCH, SEQ, D_MODEL), 1.0),
        'x_sample': nrm(ks[1], (DEC_BATCH, DEC_SEQ, D_MODEL), 1.0),
        'cache_da_k': nrm(ks[2], (DEC_BATCH, DEPTH, PAST_LEN, DA_HEADS, 2 * DA_QK), 1.0),
        'cache_da_v': nrm(ks[3], (DEC_BATCH, DEPTH, PAST_LEN, DA_HEADS, DA_V), 1.0),
        'cache_na_k': nrm(ks[4], (DEC_BATCH, DEPTH, PAST_LEN, NA_HEADS, NA_DIM), 1.0),
        'cache_na_v': nrm(ks[5], (DEC_BATCH, DEPTH, PAST_LEN, NA_HEADS, NA_DIM), 1.0),
        'state_ml_C': nrm(ks[6], (DEC_BATCH, DEPTH, 2, ML_HEADS, ML_DIM, ML_DIM), 0.1),
        'state_ml_n': nrm(ks[7], (DEC_BATCH, DEPTH, 2, ML_HEADS, ML_DIM), 0.1),
        'state_ml_m': nrm(ks[8], (DEC_BATCH, DEPTH, 2, ML_HEADS), 1.0),
        'c': nrm(ks[9], (DEC_BATCH, D_MODEL), 1.0),
        'c_ctx': nrm(ks[10], (D_MODEL,), 1.0),
        'w_mod': nrm(ks[11], (DEPTH, D_MODEL, 6 * D_MODEL), D_MODEL ** -0.5),
        'b_mod': nrm(ks[12], (DEPTH, 6 * D_MODEL), 0.02),
        'norm1': 1.0 + nrm(ks[13], (DEPTH, D_MODEL), 0.02),
        'w_in': nrm(ks[14], (DEPTH, D_MODEL, N_PROJ), D_MODEL ** -0.5),
        'b_in': b_in,
        'da_lam': nrm(ks[16], (DEPTH, 4, DA_QK), 0.1),
        'da_subln': 1.0 + nrm(ks[17], (DEPTH, DA_V), 0.02),
        'ml_norm': 1.0 + nrm(ks[18], (DEPTH, ML_W), 0.02),
        'na_rpb': nrm(ks[19], (DEPTH, NA_HEADS, 2 * NA_WIN_ROWS - 1, 2 * NA_WIN_COLS - 1), 0.02),
        'w_up_da': nrm(ks[20], (DEPTH, DA_W, D_MODEL), DA_W ** -0.5),
        'w_up_ml': nrm(ks[21], (DEPTH, ML_W, D_MODEL), ML_W ** -0.5),
        'w_up_na': nrm(ks[22], (DEPTH, NA_W, D_MODEL), NA_W ** -0.5),
        'w_out': nrm(ks[23], (DEPTH, D_MODEL, D_MODEL), D_MODEL ** -0.5),
        'norm2': 1.0 + nrm(ks[24], (DEPTH, D_MODEL), 0.02),
        'w_ff1': nrm(ks[25], (DEPTH, D_MODEL, D_FF), D_MODEL ** -0.5),
        'b_ff1': nrm(ks[26], (DEPTH, D_FF), 0.02),
        'w_ff2': nrm(ks[27], (DEPTH, D_FF, D_MODEL), D_FF ** -0.5),
        'b_ff2': nrm(ks[28], (DEPTH, D_MODEL), 0.02),
        'norm_f': 1.0 + nrm(ks[29], (D_MODEL,), 0.02),
    }


def reference(x_prompt, x_sample, cache_da_k, cache_da_v, cache_na_k, cache_na_v,
              state_ml_C, state_ml_n, state_ml_m, c, c_ctx, w_mod, b_mod, norm1, w_in, b_in,
              da_lam, da_subln, ml_norm, na_rpb, w_up_da, w_up_ml, w_up_na, w_out, norm2,
              w_ff1, b_ff1, w_ff2, b_ff2, norm_f):
    collected = [[] for _ in range(7)]
    xp, xs = x_prompt, x_sample
    for l in range(DEPTH):
        lp = {
            'norm1': norm1[l], 'w_in': w_in[l], 'b_in': b_in[l], 'da_lam': da_lam[l],
            'da_subln': da_subln[l], 'ml_norm': ml_norm[l], 'na_rpb': na_rpb[l],
            'w_up_da': w_up_da[l], 'w_up_ml': w_up_ml[l], 'w_up_na': w_up_na[l],
            'w_out': w_out[l], 'norm2': norm2[l], 'w_ff1': w_ff1[l], 'b_ff1': b_ff1[l],
            'w_ff2': w_ff2[l], 'b_ff2': b_ff2[l],
        }
        lam_init = 0.8 - 0.6 * math.exp(-0.3 * l)
        mod_ctx = (jax.nn.silu(c_ctx) @ w_mod[l] + b_mod[l])[None, None, :]
        mod_lat = (jax.nn.silu(c) @ w_mod[l] + b_mod[l])[:, None, :]
        xp, ctx_l = layer(xp, mod_ctx, lp, lam_init, None)
        cache_l = (cache_da_k[:, l], cache_da_v[:, l], cache_na_k[:, l], cache_na_v[:, l],
                   state_ml_C[:, l], state_ml_n[:, l], state_ml_m[:, l])
        xs, _ = layer(xs, mod_lat, lp, lam_init, cache_l)
        for lst, arr in zip(collected, ctx_l):
            lst.append(arr)
    y_prompt = rmsnorm(xp, norm_f)
    y_sample = rmsnorm(xs, norm_f)
    new_da_k = jnp.stack(collected[0], axis=1)
    new_da_v = jnp.stack(collected[1], axis=1)
    new_na_k = jnp.stack(collected[2], axis=1)
    new_na_v = jnp.stack(collected[3], axis=1)
    new_ml_C = jnp.stack(collected[4], axis=1)
    new_ml_n = jnp.stack(collected[5], axis=1)
    new_ml_m = jnp.stack(collected[6], axis=1)
    return (y_prompt, y_sample, new_da_k, new_da_v, new_na_k, new_na_v, new_ml_C, new_ml_n, new_ml_m)
```

```python
import functools
import math

import jax
import jax.numpy as jnp
import numpy as np
from jax import lax
from jax.experimental import pallas as pl
from jax.experimental.pallas import tpu as pltpu

F32 = jnp.float32
BF16 = jnp.bfloat16
I32 = jnp.int32

D_MODEL = 1024
BATCH = 32
SEQ = 256
DEPTH = 4
DEC_BATCH = 2
DEC_SEQ = 4096
PAST_LEN = 512
GRID_W = 64
DA_HEADS = 4
DA_QK = 64
ML_HEADS = 4
ML_DIM = 128
ML_CHUNK = 64
NA_HEADS = 8
NA_DIM = 64
NA_WIN_ROWS = 8
NA_WIN_COLS = 16
D_FF = 4 * D_MODEL
ROPE_BASE = 10000.0
EPS = 1e-6
N_MAIN = 8192
GATE_OFF = 3584
NEG = -1e30

MIB = 1024 * 1024
NT_DIMS = (((1,), (1,)), ((), ()))


def _cparams(sem, vmem_mib):
    return pltpu.CompilerParams(dimension_semantics=sem, vmem_limit_bytes=vmem_mib * MIB)


def _dot(a, b):
    return jnp.dot(a, b, preferred_element_type=F32)


def _dot_nt(a, b):
    return lax.dot_general(a, b, NT_DIMS, preferred_element_type=F32)


def _mod_kernel(c_ref, w_ref, b_ref, o_ref):
    c = c_ref[...]
    s = (c * jax.nn.sigmoid(c)).astype(BF16)
    o_ref[0] = _dot(s, w_ref[0].astype(BF16)) + b_ref[0]


def _modulation(cc, w_mod, b_mod):
    tn = 1536
    n = 6 * D_MODEL
    return pl.pallas_call(
        _mod_kernel,
        grid=(DEPTH, n // tn),
        in_specs=[
            pl.BlockSpec((8, D_MODEL), lambda l, j: (0, 0)),
            pl.BlockSpec((1, D_MODEL, tn), lambda l, j: (l, 0, j)),
            pl.BlockSpec((1, 1, tn), lambda l, j: (l, 0, j)),
        ],
        out_specs=pl.BlockSpec((1, 8, tn), lambda l, j: (l, 0, j)),
        out_shape=jax.ShapeDtypeStruct((DEPTH, 8, n), F32),
        compiler_params=_cparams(("arbitrary", "arbitrary"), 40),
        name="modulation",
    )(cc, w_mod, b_mod.reshape(DEPTH, 1, n))


def _modulated_norm(x, g, shift, scale):
    ms = jnp.mean(x * x, axis=-1, keepdims=True)
    y = x * lax.rsqrt(ms + EPS) * g
    return y * (1.0 + scale) + shift


def _inproj_kernel(*refs, rope, f32_cols):
    x_ref, mod_ref, g_ref, w_ref, b_ref, cs_ref, wg_ref, bg_ref = refs[:8]
    pos = 8
    if rope:
        cos_ref, sin_ref = refs[pos:pos + 2]
        pos += 2
    a_ref, gate_ref = refs[pos:pos + 2]
    pos += 2
    f32_refs = refs[pos:pos + len(f32_cols)]
    h_scr = refs[-1]
    j = pl.program_id(1)

    @pl.when(j == 0)
    def _():
        h = _modulated_norm(x_ref[...], g_ref[...], mod_ref[0:1, :], mod_ref[1:2, :]).astype(BF16)
        h_scr[...] = h
        gate_ref[...] = _dot(h, wg_ref[...]) + bg_ref[...]

    acc = _dot(h_scr[...], w_ref[...]) + b_ref[...]
    for ref, jj in zip(f32_refs, f32_cols):
        @pl.when(j == jj)
        def _(ref=ref):
            ref[...] = acc.reshape(ref.shape)

    acc = acc * cs_ref[...]
    if rope:
        @pl.when(j < 2)
        def _():
            lane = lax.broadcasted_iota(I32, acc.shape, 1)
            partner = jnp.where((lane & 16) == 0,
                                pltpu.roll(acc, acc.shape[1] - 16, 1),
                                pltpu.roll(acc, 16, 1))
            a_ref[...] = (acc * cos_ref[...] + partner * sin_ref[...]).astype(BF16)

        @pl.when(j >= 2)
        def _():
            a_ref[...] = acc.astype(BF16)
    else:
        a_ref[...] = acc.astype(BF16)


def _in_proj(x, modl, g1, w_main, b_main, cscale, w_g, b_g, rope_tabs, *, prompt):
    m = x.shape[0]
    tm, tn = 512, 512
    f32_cols = (1, 2, 8, 9) if prompt else ()
    if prompt:
        row_map = lambda i, j: (0, 0, 0)
    else:
        row_map = lambda i, j: (1 + i // (DEC_SEQ // tm), 0, 0)
    in_specs = [
        pl.BlockSpec((tm, D_MODEL), lambda i, j: (i, 0)),
        pl.BlockSpec((None, 6, D_MODEL), row_map),
        pl.BlockSpec((1, D_MODEL), lambda i, j: (0, 0)),
        pl.BlockSpec((D_MODEL, tn), lambda i, j: (0, j)),
        pl.BlockSpec((1, tn), lambda i, j: (0, j)),
        pl.BlockSpec((1, tn), lambda i, j: (0, j)),
        pl.BlockSpec((D_MODEL, 512), lambda i, j: (0, 0)),
        pl.BlockSpec((1, 512), lambda i, j: (0, 0)),
    ]
    args = [x, modl, g1, w_main, b_main, cscale, w_g, b_g]
    if not prompt:
        tpb = DEC_SEQ // tm
        in_specs += [pl.BlockSpec((tm, tn), lambda i, j: (i % tpb, 0))] * 2
        args += list(rope_tabs)
    out_shape = [jax.ShapeDtypeStruct((m, N_MAIN), BF16), jax.ShapeDtypeStruct((m, 512), F32)]
    out_specs = [pl.BlockSpec((tm, tn), lambda i, j: (i, j)),
                 pl.BlockSpec((tm, 512), lambda i, j: (i, 0))]
    for _ in f32_cols:
        out_shape.append(jax.ShapeDtypeStruct((m // SEQ, SEQ, tn), F32))
        out_specs.append(pl.BlockSpec((tm // SEQ, SEQ, tn), lambda i, j: (i, 0, 0)))
    return pl.pallas_call(
        functools.partial(_inproj_kernel, rope=not prompt, f32_cols=f32_cols),
        grid=(m // tm, N_MAIN // tn),
        in_specs=in_specs,
        out_specs=out_specs,
        out_shape=out_shape,
        scratch_shapes=[pltpu.VMEM((tm, D_MODEL), BF16)],
        compiler_params=_cparams(("arbitrary", "arbitrary"), 48),
        name="in_proj_ctx" if prompt else "in_proj_lat",
    )(*args)


def _softmax_pv(s_parts, vt):
    m = s_parts[0].max(axis=0, keepdims=True)
    for s in s_parts[1:]:
        m = jnp.maximum(m, s.max(axis=0, keepdims=True))
    es = [jnp.exp(s - m) for s in s_parts]
    l = es[0].sum(axis=0, keepdims=True)
    for e in es[1:]:
        l = l + e.sum(axis=0, keepdims=True)
    e = es[0] if len(es) == 1 else jnp.concatenate(es, axis=0)
    return _dot(vt, e.astype(BF16)) * (1.0 / l)


def _half_masked(q, upper):
    lane = lax.broadcasted_iota(I32, q.shape, 1)
    keep = (lane >= 64) if upper else (lane < 64)
    return jnp.where(keep, q, jnp.zeros_like(q))


def _da_kernel(*refs, hp, s_new, s_cache, lam_init):
    q_ref, k_ref, v_ref = refs[:3]
    pos = 3
    if s_cache:
        ck_ref, cv_ref = refs[pos:pos + 2]
        pos += 2
    lam_ref, sub_ref, o_ref, k_scr, vt_scr = refs[pos:pos + 5]
    qi = pl.program_id(2)

    @pl.when(qi == 0)
    def _():
        for hh in range(hp):
            sl = slice(hh * 128, (hh + 1) * 128)
            k_scr[hh, 0:s_new, :] = k_ref[:, sl]
            for c0 in range(0, s_new, 256):
                vt_scr[hh, :, c0:c0 + 256] = v_ref[c0:c0 + 256, sl].astype(F32).T.astype(BF16)
            if s_cache:
                k_scr[hh, s_new:s_new + s_cache, :] = ck_ref[:, sl].astype(BF16)
                for c0 in range(0, s_cache, 256):
                    vt_scr[hh, :, s_new + c0:s_new + c0 + 256] = cv_ref[c0:c0 + 256, sl].T.astype(BF16)

    lv = lam_ref[...]
    lam = (jnp.exp(jnp.sum(lv[0:1] * lv[1:2], axis=1, keepdims=True))
           - jnp.exp(jnp.sum(lv[2:3] * lv[3:4], axis=1, keepdims=True)) + lam_init)
    for hh in range(hp):
        sl = slice(hh * 128, (hh + 1) * 128)
        q = q_ref[:, sl]
        k = k_scr[hh]
        vt = vt_scr[hh]
        o1 = _softmax_pv([_dot_nt(k, _half_masked(q, False))], vt)
        o2 = _softmax_pv([_dot_nt(k, _half_masked(q, True))], vt)
        o = o1 - lam * o2
        ms = jnp.mean(o * o, axis=0, keepdims=True)
        on = o * lax.rsqrt(ms + EPS) * sub_ref[...] * (1.0 - lam_init)
        o_ref[:, sl] = on.T.astype(BF16)


def _diff_attention(a, cache_k, cache_v, l, da_lam_l, subln_col, lam_init, *, prompt):
    m = a.shape[0]
    if prompt:
        hp, tq, s_new, s_cache = DA_HEADS, SEQ, SEQ, 0
        grid = (BATCH, 1, 1)
        in_specs = [
            pl.BlockSpec((SEQ, 512), lambda b, h, qi: (b, 0)),
            pl.BlockSpec((SEQ, 512), lambda b, h, qi: (b, 1)),
            pl.BlockSpec((SEQ, 512), lambda b, h, qi: (b, 2)),
        ]
        args = [a, a, a]
        out_spec = pl.BlockSpec((SEQ, 512), lambda b, h, qi: (b, 0))
    else:
        hp, tq, s_new, s_cache = 1, 256, DEC_SEQ, PAST_LEN
        nq = DEC_SEQ // tq
        grid = (DEC_BATCH, DA_HEADS, nq)
        in_specs = [
            pl.BlockSpec((tq, 128), lambda b, h, qi: (b * nq + qi, h)),
            pl.BlockSpec((DEC_SEQ, 128), lambda b, h, qi: (b, 4 + h)),
            pl.BlockSpec((DEC_SEQ, 128), lambda b, h, qi: (b, 8 + h)),
            pl.BlockSpec((None, None, PAST_LEN, 128), lambda b, h, qi: (b, l, 0, h)),
            pl.BlockSpec((None, None, PAST_LEN, 128), lambda b, h, qi: (b, l, 0, h)),
        ]
        args = [a, a, a, cache_k, cache_v]
        out_spec = pl.BlockSpec((tq, 128), lambda b, h, qi: (b * nq + qi, h))
    in_specs += [
        pl.BlockSpec((4, DA_QK), lambda b, h, qi: (0, 0)),
        pl.BlockSpec((128, 1), lambda b, h, qi: (0, 0)),
    ]
    args += [da_lam_l, subln_col]
    s_all = s_new + s_cache
    return pl.pallas_call(
        functools.partial(_da_kernel, hp=hp, s_new=s_new, s_cache=s_cache, lam_init=lam_init),
        grid=grid,
        in_specs=in_specs,
        out_specs=out_spec,
        out_shape=jax.ShapeDtypeStruct((m, 512), BF16),
        scratch_shapes=[pltpu.VMEM((hp, s_all, 128), BF16), pltpu.VMEM((hp, 128, s_all), BF16)],
        compiler_params=_cparams(("arbitrary", "arbitrary", "arbitrary"), 48),
        name="diff_attn_ctx" if prompt else "diff_attn_lat",
    )(*args)


def _na_bias_kernel(rpb_ref, o_ref):
    h = pl.program_id(0)
    n_dc = 2 * NA_WIN_COLS - 1
    n_dr = 2 * NA_WIN_ROWS - 1
    kc = lax.broadcasted_iota(I32, (GRID_W, 128), 0)
    lane = lax.broadcasted_iota(I32, (GRID_W, 128), 1)
    qc = lane & (GRID_W - 1)
    dcm = kc - qc + (NA_WIN_COLS - 1)
    cstart = jnp.clip(qc - NA_WIN_COLS // 2, 0, GRID_W - NA_WIN_COLS)
    left = lane < GRID_W
    base = h * (n_dr * n_dc)
    for d in range(16):
        acc = jnp.zeros((GRID_W, 128), F32)
        for dc in range(n_dc):
            lv = rpb_ref[base + d * n_dc + dc] if d < n_dr else 0.0
            rv = rpb_ref[base + (d - 1) * n_dc + dc] if d >= 1 else 0.0
            acc = jnp.where(dcm == dc, jnp.where(left, lv, rv), acc)
        o_ref[0, d] = jnp.where(kc >= cstart, jnp.where(kc < cstart + NA_WIN_COLS, acc, NEG), NEG)


def _na_bias_table(rpb_l):
    return pl.pallas_call(
        _na_bias_kernel,
        grid=(NA_HEADS,),
        in_specs=[pl.BlockSpec(memory_space=pltpu.SMEM)],
        out_specs=pl.BlockSpec((1, 16, GRID_W, 128), lambda h: (h, 0, 0, 0)),
        out_shape=jax.ShapeDtypeStruct((NA_HEADS, 16, GRID_W, 128), F32),
        compiler_params=_cparams(("arbitrary",), 16),
        name="na_bias_table",
    )(rpb_l.reshape(-1))


def _na_lat_kernel(q_ref, k_ref, v_ref, ck_ref, cv_ref, bp_ref, o_ref):
    rb = pl.program_id(1)
    ws = jnp.clip(2 * rb - 1, 0, 12)
    delta = 4 * ws - 8 * rb
    tok0 = pl.multiple_of(ws * 256, 256)
    kr = 4 * ws + (lax.broadcasted_iota(I32, (1024, 512), 0) >> 6)
    qr = 8 * rb + (lax.broadcasted_iota(I32, (1024, 512), 1) >> 6)
    st = jnp.clip(qr - NA_WIN_ROWS // 2, 0, GRID_W - NA_WIN_ROWS)
    rowmask = jnp.where(kr >= st, jnp.where(kr < st + NA_WIN_ROWS, 0.0, NEG), NEG)
    for g in range(NA_HEADS // 2):
        sl = slice(g * 128, (g + 1) * 128)
        q2 = q_ref[:, sl]
        kall = jnp.concatenate([k_ref[pl.ds(tok0, 1024), sl], ck_ref[:, sl].astype(BF16)], axis=0)
        vall = jnp.concatenate([v_ref[pl.ds(tok0, 1024), sl].astype(F32), cv_ref[:, sl]], axis=0)
        vt = vall.T.astype(BF16)
        outs = []
        for par in range(2):
            h = 2 * g + par
            st_all = _dot_nt(kall, _half_masked(q2, par == 1))
            rows = []
            for krl in range(16):
                tiles = []
                for jq in range(4):
                    d = jnp.clip(krl - 2 * jq + 7 + delta, 0, 15)
                    tiles.append(bp_ref[h, d])
                rows.append(jnp.concatenate(tiles, axis=1))
            bias = jnp.concatenate(rows, axis=0)
            s_loc = st_all[:1024] + bias + rowmask
            outs.append(_softmax_pv([s_loc, st_all[1024:]], vt))
        o_t = jnp.concatenate([outs[0][:64], outs[1][64:]], axis=0)
        o_ref[:, sl] = o_t.T.astype(BF16)


def _na_lat(a, cache_k, cache_v, l, bias_tab):
    return pl.pallas_call(
        _na_lat_kernel,
        grid=(DEC_BATCH, 8),
        in_specs=[
            pl.BlockSpec((512, 512), lambda b, rb: (b * 8 + rb, 7)),
            pl.BlockSpec((DEC_SEQ, 512), lambda b, rb: (b, 8)),
            pl.BlockSpec((DEC_SEQ, 512), lambda b, rb: (b, 9)),
            pl.BlockSpec((None, None, PAST_LEN, 512), lambda b, rb: (b, l, 0, 0)),
            pl.BlockSpec((None, None, PAST_LEN, 512), lambda b, rb: (b, l, 0, 0)),
            pl.BlockSpec((NA_HEADS, 16, GRID_W, 128), lambda b, rb: (0, 0, 0, 0)),
        ],
        out_specs=pl.BlockSpec((512, 512), lambda b, rb: (b * 8 + rb, 0)),
        out_shape=jax.ShapeDtypeStruct((DEC_BATCH * DEC_SEQ, 512), BF16),
        compiler_params=_cparams(("arbitrary", "arbitrary"), 56),
        name="nbr_attn_lat",
    )(a, a, a, cache_k, cache_v, bias_tab)


def _na_ctx_kernel(q_ref, k_ref, v_ref, o_ref):
    for g in range(NA_HEADS // 2):
        sl = slice(g * 128, (g + 1) * 128)
        q2 = q_ref[:, sl]
        k2 = k_ref[:, sl]
        vt = v_ref[:, sl].astype(F32).T.astype(BF16)
        o_a = _softmax_pv([_dot_nt(k2, _half_masked(q2, False))], vt)
        o_b = _softmax_pv([_dot_nt(k2, _half_masked(q2, True))], vt)
        o_t = jnp.concatenate([o_a[:64], o_b[64:]], axis=0)
        o_ref[:, sl] = o_t.T.astype(BF16)


def _na_ctx(a):
    return pl.pallas_call(
        _na_ctx_kernel,
        grid=(BATCH,),
        in_specs=[
            pl.BlockSpec((SEQ, 512), lambda b: (b, 7)),
            pl.BlockSpec((SEQ, 512), lambda b: (b, 8)),
            pl.BlockSpec((SEQ, 512), lambda b: (b, 9)),
        ],
        out_specs=pl.BlockSpec((SEQ, 512), lambda b: (b, 0)),
        out_shape=jax.ShapeDtypeStruct((BATCH * SEQ, 512), BF16),
        compiler_params=_cparams(("arbitrary",), 32),
        name="nbr_attn_ctx",
    )(a, a, a)


def _log_sigmoid(x):
    return jnp.minimum(x, 0.0) - jnp.log1p(jnp.exp(-jnp.abs(x)))


def _split3(x):
    x1 = x.astype(BF16)
    r1 = x - x1.astype(F32)
    x2 = r1.astype(BF16)
    x3 = (r1 - x2.astype(F32)).astype(BF16)
    return x1, x2, x3


def _ml_kernel(*refs, t_len, zero_init, emit_state):
    q_ref, k_ref, v_ref, og_ref, g_ref = refs[:5]
    pos = 5
    if not zero_init:
        c0_ref, n0_ref, m0_ref = refs[pos:pos + 3]
        pos += 3
    nrm_ref, o_ref = refs[pos:pos + 2]
    pos += 2
    if emit_state:
        cst_ref, nst_ref, mst_ref = refs[pos:pos + 3]
        pos += 3
    hf_scr, hb_scr, c_scr = refs[pos:pos + 3]
    b = pl.program_id(0)
    h = pl.program_id(1)
    lc = ML_CHUNK
    nc = t_len // lc

    ri = lax.broadcasted_iota(I32, (lc, lc), 0)
    ci = lax.broadcasted_iota(I32, (lc, lc), 1)
    lo = ri >= ci
    up = ri <= ci
    lo_b = jnp.where(lo, 1.0, 0.0).astype(BF16)
    up_b = jnp.where(up, 1.0, 0.0).astype(BF16)

    def tri_left(tri, x):
        return sum(_dot(tri, p) for p in _split3(x))

    def tri_right(x, tri):
        return sum(_dot(p, tri) for p in _split3(x))

    if zero_init:
        c_scr[...] = jnp.zeros_like(c_scr)
        n_init = [jnp.zeros((1, ML_DIM), F32)] * 2
        m_init = [jnp.zeros((1, 1), F32)] * 2
    else:
        c_scr[...] = c0_ref[...]
        n_init = [n0_ref[dr, pl.ds(h, 1), :] for dr in range(2)]
        m_init = [jnp.full((1, 1), m0_ref[(b * 2 + dr) * ML_HEADS + h], F32) for dr in range(2)]

    def chunk(dr, c, n_st, mm):
        r0 = pl.multiple_of(c * lc, lc)
        qc = q_ref[pl.ds(r0, lc), :]
        kc = k_ref[pl.ds(r0, lc), :]
        vc = v_ref[pl.ds(r0, lc), :]
        g = g_ref[pl.ds(r0, lc), :]
        gt = g.T[0:8]
        lf = _log_sigmoid(g)
        lft = _log_sigmoid(gt)
        if dr == 0:
            cum, cumt = tri_left(lo_b, lf), tri_right(lft, up_b)
            icol, fcol, mask, last = 0, 1, lo, lc - 1
        else:
            cum, cumt = tri_left(up_b, lf), tri_right(lft, lo_b)
            icol, fcol, mask, last = 2, 3, up, 0
        bc = cum[:, fcol:fcol + 1]
        bct = cumt[fcol:fcol + 1, :]
        ic = g[:, icol:icol + 1]
        ict = gt[icol:icol + 1, :]
        logd = jnp.where(mask, bc - (bct - ict), -jnp.inf)
        mt = jnp.maximum(bc + mm, logd.max(axis=1, keepdims=True))
        inter = jnp.exp(bc + mm - mt)
        sc = _dot_nt(qc, kc) * jnp.exp(logd - mt)
        c_old = c_scr[dr]
        num = inter * _dot(qc, c_old.astype(BF16)) + _dot(sc.astype(BF16), vc)
        den = inter * jnp.sum(qc.astype(F32) * n_st, axis=1, keepdims=True) + sc.sum(axis=1, keepdims=True)
        hc = num / jnp.maximum(jnp.abs(den), jnp.exp(-mt))
        m_new = mt[last:last + 1, :]
        bl = bc[last:last + 1, :]
        w = jnp.exp(bl - bc + ic - m_new)
        decay = jnp.exp(bl + mm - m_new)
        kf = kc.astype(F32)
        kt = kf.T.astype(BF16)
        c_scr[dr] = decay * c_old + _dot(kt, (w * vc.astype(F32)).astype(BF16))
        n_new = decay * n_st + jnp.sum(w * kf, axis=0, keepdims=True)
        (hf_scr if dr == 0 else hb_scr)[pl.ds(r0, lc), :] = hc
        return n_new, m_new

    def body(i, carry):
        nf, mf, nb, mb = carry
        nf, mf = chunk(0, i, nf, mf)
        nb, mb = chunk(1, nc - 1 - i, nb, mb)
        return nf, mf, nb, mb

    nf, mf, nb, mb = lax.fori_loop(0, nc, body, (n_init[0], m_init[0], n_init[1], m_init[1]))

    hs = hf_scr[...] + hb_scr[...]
    ms = jnp.mean(hs * hs, axis=-1, keepdims=True)
    y = hs * lax.rsqrt(ms + EPS) * nrm_ref[...]
    o_ref[...] = (y * jax.nn.sigmoid(og_ref[...].astype(F32))).astype(BF16)
    if emit_state:
        cst_ref[...] = c_scr[...]
        nst_ref[0:1, :] = nf
        nst_ref[1:2, :] = nb
        mst_ref[0:1, :] = jnp.broadcast_to(mf, (1, 128))
        mst_ref[1:2, :] = jnp.broadcast_to(mb, (1, 128))


def _mlstm(a, gates, states, l, ml_norm_l, *, prompt):
    m = a.shape[0]
    t_len = SEQ if prompt else DEC_SEQ
    nb = m // t_len
    in_specs = [
        pl.BlockSpec((t_len, 128), lambda b, h: (b, 12 + h)),
        pl.BlockSpec((t_len, 128), lambda b, h: (b, 16 + h)),
        pl.BlockSpec((t_len, 128), lambda b, h: (b, 20 + h)),
        pl.BlockSpec((t_len, 128), lambda b, h: (b, 24 + h)),
        pl.BlockSpec((t_len, 128), lambda b, h: (b, h)),
    ]
    args = [a, a, a, a, gates]
    if not prompt:
        c0, n0, m0 = states
        in_specs += [
            pl.BlockSpec((None, None, 2, None, ML_DIM, ML_DIM), lambda b, h: (b, l, 0, h, 0, 0)),
            pl.BlockSpec((None, None, 2, ML_HEADS, ML_DIM), lambda b, h: (b, l, 0, 0, 0)),
            pl.BlockSpec(memory_space=pltpu.SMEM),
        ]
        args += [c0, n0, m0[:, l].reshape(-1)]
    in_specs.append(pl.BlockSpec((None, 1, ML_DIM), lambda b, h: (h, 0, 0)))
    args.append(ml_norm_l.reshape(ML_HEADS, 1, ML_DIM))
    out_shape = [jax.ShapeDtypeStruct((m, 512), BF16)]
    out_specs = [pl.BlockSpec((t_len, 128), lambda b, h: (b, h))]
    if prompt:
        out_shape += [
            jax.ShapeDtypeStruct((nb, 2, ML_HEADS, ML_DIM, ML_DIM), F32),
            jax.ShapeDtypeStruct((nb, ML_HEADS, 2, ML_DIM), F32),
            jax.ShapeDtypeStruct((nb, ML_HEADS, 2, 128), F32),
        ]
        out_specs += [
            pl.BlockSpec((None, 2, None, ML_DIM, ML_DIM), lambda b, h: (b, 0, h, 0, 0)),
            pl.BlockSpec((None, None, 2, ML_DIM), lambda b, h: (b, h, 0, 0)),
            pl.BlockSpec((None, None, 2, 128), lambda b, h: (b, h, 0, 0)),
        ]
    return pl.pallas_call(
        functools.partial(_ml_kernel, t_len=t_len, zero_init=prompt, emit_state=prompt),
        grid=(nb, ML_HEADS),
        in_specs=in_specs,
        out_specs=out_specs,
        out_shape=out_shape,
        scratch_shapes=[pltpu.VMEM((t_len, ML_DIM), F32), pltpu.VMEM((t_len, ML_DIM), F32),
                        pltpu.VMEM((2, ML_DIM, ML_DIM), F32)],
        compiler_params=_cparams(("arbitrary", "arbitrary"), 40),
        name="mlstm_ctx" if prompt else "mlstm_lat",
    )(*args)


def _merge_kernel(x_ref, mod_ref, oda_ref, oml_ref, ona_ref, g0_ref, g1_ref, g2_ref,
                  wda_ref, wml_ref, wna_ref, wo_ref, o_ref):
    def branch(o, w, g):
        return jax.nn.sigmoid(g[...].astype(F32)) * _dot(o[...], w[...])

    merged = (branch(oda_ref, wda_ref, g0_ref) + branch(oml_ref, wml_ref, g1_ref)
              + branch(ona_ref, wna_ref, g2_ref))
    o_ref[...] = x_ref[...] + mod_ref[2:3, :] * _dot(merged.astype(BF16), wo_ref[...])


def _mod_row_map(prompt, tm):
    if prompt:
        return lambda i, *_: (0, 0, 0)
    return lambda i, *_: (1 + i // (DEC_SEQ // tm), 0, 0)


def _merge_out(x, modl, o_da, o_ml, o_na, a, w_da, w_ml, w_na, w_out, *, prompt):
    m = x.shape[0]
    tm = 512
    const = lambda i: (0, 0)
    return pl.pallas_call(
        _merge_kernel,
        grid=(m // tm,),
        in_specs=[
            pl.BlockSpec((tm, D_MODEL), lambda i: (i, 0)),
            pl.BlockSpec((None, 6, D_MODEL), _mod_row_map(prompt, tm)),
            pl.BlockSpec((tm, 512), lambda i: (i, 0)),
            pl.BlockSpec((tm, 512), lambda i: (i, 0)),
            pl.BlockSpec((tm, 512), lambda i: (i, 0)),
            pl.BlockSpec((tm, D_MODEL), lambda i: (i, 5)),
            pl.BlockSpec((tm, D_MODEL), lambda i: (i, 6)),
            pl.BlockSpec((tm, D_MODEL), lambda i: (i, 7)),
            pl.BlockSpec((512, D_MODEL), const),
            pl.BlockSpec((512, D_MODEL), const),
            pl.BlockSpec((512, D_MODEL), const),
            pl.BlockSpec((D_MODEL, D_MODEL), const),
        ],
        out_specs=pl.BlockSpec((tm, D_MODEL), lambda i: (i, 0)),
        out_shape=jax.ShapeDtypeStruct((m, D_MODEL), F32),
        compiler_params=_cparams(("arbitrary",), 48),
        name="merge_out_ctx" if prompt else "merge_out_lat",
    )(x, modl, o_da, o_ml, o_na, a, a, a, w_da, w_ml, w_na, w_out)


def _ffn_kernel(x_ref, mod_ref, g_ref, w1_ref, b1_ref, w2_ref, b2_ref, o_ref, h_scr, acc_scr):
    f = pl.program_id(1)

    @pl.when(f == 0)
    def _():
        h_scr[...] = _modulated_norm(x_ref[...], g_ref[...], mod_ref[3:4, :], mod_ref[4:5, :]).astype(BF16)
        acc_scr[...] = jnp.zeros_like(acc_scr)

    u = jnp.maximum(_dot(h_scr[...], w1_ref[...]) + b1_ref[...], 0.0)
    acc_scr[...] += _dot((u * u).astype(BF16), w2_ref[...])

    @pl.when(f == pl.num_programs(1) - 1)
    def _():
        o_ref[...] = x_ref[...] + mod_ref[5:6, :] * (acc_scr[...] + b2_ref[...])


def _ffn(x, modl, g2, w1, b1, w2, b2, *, prompt):
    m = x.shape[0]
    tm, tf = 512, 1024
    return pl.pallas_call(
        _ffn_kernel,
        grid=(m // tm, D_FF // tf),
        in_specs=[
            pl.BlockSpec((tm, D_MODEL), lambda i, f: (i, 0)),
            pl.BlockSpec((None, 6, D_MODEL), _mod_row_map(prompt, tm)),
            pl.BlockSpec((1, D_MODEL), lambda i, f: (0, 0)),
            pl.BlockSpec((D_MODEL, tf), lambda i, f: (0, f)),
            pl.BlockSpec((1, tf), lambda i, f: (0, f)),
            pl.BlockSpec((tf, D_MODEL), lambda i, f: (f, 0)),
            pl.BlockSpec((1, D_MODEL), lambda i, f: (0, 0)),
        ],
        out_specs=pl.BlockSpec((tm, D_MODEL), lambda i, f: (i, 0)),
        out_shape=jax.ShapeDtypeStruct((m, D_MODEL), F32),
        scratch_shapes=[pltpu.VMEM((tm, D_MODEL), BF16), pltpu.VMEM((tm, D_MODEL), F32)],
        compiler_params=_cparams(("arbitrary", "arbitrary"), 48),
        name="ffn_ctx" if prompt else "ffn_lat",
    )(x, modl, g2, w1, b1, w2, b2)


def _final_norm_kernel(x_ref, g_ref, o_ref):
    x = x_ref[...]
    ms = jnp.mean(x * x, axis=-1, keepdims=True)
    o_ref[...] = x * lax.rsqrt(ms + EPS) * g_ref[...]


def _final_norm(x, g):
    m = x.shape[0]
    tm = 1024
    return pl.pallas_call(
        _final_norm_kernel,
        grid=(m // tm,),
        in_specs=[pl.BlockSpec((tm, D_MODEL), lambda i: (i, 0)),
                  pl.BlockSpec((1, D_MODEL), lambda i: (0, 0))],
        out_specs=pl.BlockSpec((tm, D_MODEL), lambda i: (i, 0)),
        out_shape=jax.ShapeDtypeStruct((m, D_MODEL), F32),
        compiler_params=_cparams(("arbitrary",), 32),
        name="final_norm",
    )(x, g)


def _rope_tables():
    t = jnp.arange(DEC_SEQ)
    lane = jnp.arange(512)
    sub = lane % 32
    freq = ROPE_BASE ** (-(2 * (sub % 16)).astype(F32) / 32.0)
    use_row = (lane % 64) < 32
    posv = jnp.where(use_row[None, :], (t // GRID_W)[:, None], (t % GRID_W)[:, None]).astype(F32)
    ang = posv * freq[None, :]
    sign = jnp.where(sub < 16, -1.0, 1.0).astype(F32)
    return jnp.cos(ang), jnp.sin(ang) * sign[None, :]


def kernel(x_prompt, x_sample, cache_da_k, cache_da_v, cache_na_k, cache_na_v, state_ml_C, state_ml_n,
           state_ml_m, c, c_ctx, w_mod, b_mod, norm1, w_in, b_in, da_lam, da_subln, ml_norm, na_rpb,
           w_up_da, w_up_ml, w_up_na, w_out, norm2, w_ff1, b_ff1, w_ff2, b_ff2, norm_f):
    xp = x_prompt.reshape(BATCH * SEQ, D_MODEL)
    xs = x_sample.reshape(DEC_BATCH * DEC_SEQ, D_MODEL)

    w_main = jnp.concatenate([w_in[..., :GATE_OFF], w_in[..., GATE_OFF + 16:]], axis=-1).astype(BF16)
    b_main = jnp.concatenate([b_in[..., :GATE_OFF], b_in[..., GATE_OFF + 16:]], axis=-1).reshape(DEPTH, 1, N_MAIN)
    src, dst = [], []
    for hh in range(ML_HEADS):
        for dr in range(2):
            for tt in range(2):
                src.append(GATE_OFF + dr * 2 * ML_HEADS + tt * ML_HEADS + hh)
                dst.append(hh * 128 + dr * 2 + tt)
    src, dst = np.array(src), np.array(dst)
    w_g = jnp.zeros((DEPTH, D_MODEL, 512), F32).at[:, :, dst].set(w_in[:, :, src]).astype(BF16)
    b_g = jnp.zeros((DEPTH, 1, 512), F32).at[:, 0, dst].set(b_in[:, src])
    cs = np.ones((1, N_MAIN), np.float32)
    cs[0, 0:512] = DA_QK ** -0.5
    cs[0, 2048:2560] = ML_DIM ** -0.5
    cs[0, 3584:4096] = NA_DIM ** -0.5
    cscale = jnp.asarray(cs)
    w_da_b, w_ml_b, w_na_b, w_out_b = (w.astype(BF16) for w in (w_up_da, w_up_ml, w_up_na, w_out))
    w_ff1_b, w_ff2_b = w_ff1.astype(BF16), w_ff2.astype(BF16)
    rope_tabs = _rope_tables()

    cc = jnp.zeros((8, D_MODEL), F32).at[0].set(c_ctx).at[1:1 + DEC_BATCH].set(c)
    mod = _modulation(cc, w_mod, b_mod).reshape(DEPTH, 8, 6, D_MODEL)

    cdk = cache_da_k.reshape(DEC_BATCH, DEPTH, PAST_LEN, 512)
    cdv = cache_da_v.reshape(DEC_BATCH, DEPTH, PAST_LEN, 512)
    cnk = cache_na_k.reshape(DEC_BATCH, DEPTH, PAST_LEN, 512)
    cnv = cache_na_v.reshape(DEC_BATCH, DEPTH, PAST_LEN, 512)

    coll = [[] for _ in range(7)]
    for l in range(DEPTH):
        lam_init = 0.8 - 0.6 * math.exp(-0.3 * l)
        modl = mod[l]
        g1 = norm1[l].reshape(1, D_MODEL)
        g2 = norm2[l].reshape(1, D_MODEL)
        sub_col = da_subln[l].reshape(128, 1)
        bias_tab = _na_bias_table(na_rpb[l])
        for prompt in (True, False):
            x = xp if prompt else xs
            outs = _in_proj(x, modl, g1, w_main[l], b_main[l], cscale, w_g[l], b_g[l], rope_tabs, prompt=prompt)
            a, gates = outs[0], outs[1]
            if prompt:
                for lst, arr in zip(coll[:4], outs[2:6]):
                    lst.append(arr)
            o_da = _diff_attention(a, cdk, cdv, l, da_lam[l], sub_col, lam_init, prompt=prompt)
            ml_out = _mlstm(a, gates, (state_ml_C, state_ml_n, state_ml_m), l, ml_norm[l], prompt=prompt)
            o_ml = ml_out[0]
            if prompt:
                coll[4].append(ml_out[1])
                coll[5].append(jnp.swapaxes(ml_out[2], 1, 2))
                coll[6].append(jnp.swapaxes(ml_out[3][..., 0], 1, 2))
                o_na = _na_ctx(a)
            else:
                o_na = _na_lat(a, cnk, cnv, l, bias_tab)
            x = _merge_out(x, modl, o_da, o_ml, o_na, a, w_da_b[l], w_ml_b[l], w_na_b[l], w_out_b[l],
                           prompt=prompt)
            x = _ffn(x, modl, g2, w_ff1_b[l], b_ff1[l].reshape(1, D_FF), w_ff2_b[l],
                     b_ff2[l].reshape(1, D_MODEL), prompt=prompt)
            if prompt:
                xp = x
            else:
                xs = x

    gf = norm_f.reshape(1, D_MODEL)
    y_prompt = _final_norm(xp, gf).reshape(BATCH, SEQ, D_MODEL)
    y_sample = _final_norm(xs, gf).reshape(DEC_BATCH, DEC_SEQ, D_MODEL)
    new_da_k = jnp.stack(coll[0], axis=1).reshape(BATCH, DEPTH, SEQ, DA_HEADS, 2 * DA_QK)
    new_da_v = jnp.stack(coll[1], axis=1).reshape(BATCH, DEPTH, SEQ, DA_HEADS, 2 * DA_QK)
    new_na_k = jnp.stack(coll[2], axis=1).reshape(BATCH, DEPTH, SEQ, NA_HEADS, NA_DIM)
    new_na_v = jnp.stack(coll[3], axis=1).reshape(BATCH, DEPTH, SEQ, NA_HEADS, NA_DIM)
    new_ml_c = jnp.stack(coll[4], axis=1)
    new_ml_n = jnp.stack(coll[5], axis=1)
    new_ml_m = jnp.stack(coll[6], axis=1)
    return (y_prompt, y_sample, new_da_k, new_da_v, new_na_k, new_na_v, new_ml_c, new_ml_n, new_ml_m)
```

```python
import functools
import math

import jax
import jax.numpy as jnp
import numpy as np
from jax import lax
from jax.experimental import pallas as pl
from jax.experimental.pallas import tpu as pltpu

F32 = jnp.float32
BF16 = jnp.bfloat16
I32 = jnp.int32

D_MODEL = 1024
BATCH = 32
SEQ = 256
DEPTH = 4
DEC_BATCH = 2
DEC_SEQ = 4096
PAST_LEN = 512
GRID_W = 64
DA_HEADS = 4
DA_QK = 64
ML_HEADS = 4
ML_DIM = 128
ML_LC = 128
NA_HEADS = 8
NA_DIM = 64
NA_WIN_ROWS = 8
NA_WIN_COLS = 16
D_FF = 4 * D_MODEL
ROPE_BASE = 10000.0
EPS = 1e-6
N_MAIN = 8192
GATE_OFF = 3584
NEG = -1e30

MIB = 1024 * 1024
NT_DIMS = (((1,), (1,)), ((), ()))


def _cparams(sem, vmem_mib):
    return pltpu.CompilerParams(dimension_semantics=sem, vmem_limit_bytes=vmem_mib * MIB)


def _dot(a, b):
    return jnp.dot(a, b, preferred_element_type=F32)


def _dot_nt(a, b):
    return lax.dot_general(a, b, NT_DIMS, preferred_element_type=F32)


def _mod_kernel(c_ref, w_ref, b_ref, o_ref):
    c = c_ref[...]
    s = (c * jax.nn.sigmoid(c)).astype(BF16)
    o_ref[0] = _dot(s, w_ref[0].astype(BF16)) + b_ref[0]


def _modulation(cc, w_mod, b_mod):
    tn = 1536
    n = 6 * D_MODEL
    return pl.pallas_call(
        _mod_kernel,
        grid=(DEPTH, n // tn),
        in_specs=[
            pl.BlockSpec((8, D_MODEL), lambda l, j: (0, 0)),
            pl.BlockSpec((1, D_MODEL, tn), lambda l, j: (l, 0, j)),
            pl.BlockSpec((1, 1, tn), lambda l, j: (l, 0, j)),
        ],
        out_specs=pl.BlockSpec((1, 8, tn), lambda l, j: (l, 0, j)),
        out_shape=jax.ShapeDtypeStruct((DEPTH, 8, n), F32),
        compiler_params=_cparams(("arbitrary", "arbitrary"), 40),
        name="modulation",
    )(cc, w_mod, b_mod.reshape(DEPTH, 1, n))


def _modulated_norm(x, g, shift, scale):
    ms = jnp.mean(x * x, axis=-1, keepdims=True)
    y = x * lax.rsqrt(ms + EPS) * g
    return y * (1.0 + scale) + shift


def _inproj_kernel(*refs, rope, f32_cols):
    x_ref, mod_ref, g_ref, w_ref, b_ref, cs_ref, wg_ref, bg_ref = refs[:8]
    pos = 8
    if rope:
        cos_ref, sin_ref = refs[pos:pos + 2]
        pos += 2
    a_ref, gate_ref = refs[pos:pos + 2]
    pos += 2
    f32_refs = refs[pos:pos + len(f32_cols)]
    h_scr = refs[-1]
    j = pl.program_id(1)

    @pl.when(j == 0)
    def _():
        h = _modulated_norm(x_ref[...], g_ref[...], mod_ref[0:1, :], mod_ref[1:2, :]).astype(BF16)
        h_scr[...] = h
        gate_ref[...] = _dot(h, wg_ref[...]) + bg_ref[...]

    acc = _dot(h_scr[...], w_ref[...]) + b_ref[...]
    for ref, jj in zip(f32_refs, f32_cols):
        @pl.when(j == jj)
        def _(ref=ref):
            ref[...] = acc.reshape(ref.shape)

    acc = acc * cs_ref[...]
    if rope:
        @pl.when(j < 2)
        def _():
            lane = lax.broadcasted_iota(I32, acc.shape, 1)
            partner = jnp.where((lane & 16) == 0,
                                pltpu.roll(acc, acc.shape[1] - 16, 1),
                                pltpu.roll(acc, 16, 1))
            a_ref[...] = (acc * cos_ref[...] + partner * sin_ref[...]).astype(BF16)

        @pl.when(j >= 2)
        def _():
            a_ref[...] = acc.astype(BF16)
    else:
        a_ref[...] = acc.astype(BF16)


def _in_proj(x, modl, g1, w_main, b_main, cscale, w_g, b_g, rope_tabs, *, prompt):
    m = x.shape[0]
    tm, tn = 1024, 512
    f32_cols = (1, 2, 8, 9) if prompt else ()
    if prompt:
        row_map = lambda i, j: (0, 0, 0)
    else:
        row_map = lambda i, j: (1 + i // (DEC_SEQ // tm), 0, 0)
    in_specs = [
        pl.BlockSpec((tm, D_MODEL), lambda i, j: (i, 0)),
        pl.BlockSpec((None, 6, D_MODEL), row_map),
        pl.BlockSpec((1, D_MODEL), lambda i, j: (0, 0)),
        pl.BlockSpec((D_MODEL, tn), lambda i, j: (0, j)),
        pl.BlockSpec((1, tn), lambda i, j: (0, j)),
        pl.BlockSpec((1, tn), lambda i, j: (0, j)),
        pl.BlockSpec((D_MODEL, 128), lambda i, j: (0, 0)),
        pl.BlockSpec((1, 128), lambda i, j: (0, 0)),
    ]
    args = [x, modl, g1, w_main, b_main, cscale, w_g, b_g]
    if not prompt:
        tpb = DEC_SEQ // tm
        in_specs += [pl.BlockSpec((tm, tn), lambda i, j: (i % tpb, 0))] * 2
        args += list(rope_tabs)
    out_shape = [jax.ShapeDtypeStruct((m, N_MAIN), BF16), jax.ShapeDtypeStruct((m, 128), F32)]
    out_specs = [pl.BlockSpec((tm, tn), lambda i, j: (i, j)),
                 pl.BlockSpec((tm, 128), lambda i, j: (i, 0))]
    for _ in f32_cols:
        out_shape.append(jax.ShapeDtypeStruct((m // SEQ, SEQ, tn), F32))
        out_specs.append(pl.BlockSpec((tm // SEQ, SEQ, tn), lambda i, j: (i, 0, 0)))
    return pl.pallas_call(
        functools.partial(_inproj_kernel, rope=not prompt, f32_cols=f32_cols),
        grid=(m // tm, N_MAIN // tn),
        in_specs=in_specs,
        out_specs=out_specs,
        out_shape=out_shape,
        scratch_shapes=[pltpu.VMEM((tm, D_MODEL), BF16)],
        compiler_params=_cparams(("arbitrary", "arbitrary"), 48),
        name="in_proj_ctx" if prompt else "in_proj_lat",
    )(*args)


def _softmax_pv(s_parts, vt):
    m = s_parts[0].max(axis=0, keepdims=True)
    for s in s_parts[1:]:
        m = jnp.maximum(m, s.max(axis=0, keepdims=True))
    es = [jnp.exp(s - m) for s in s_parts]
    l = es[0].sum(axis=0, keepdims=True)
    for e in es[1:]:
        l = l + e.sum(axis=0, keepdims=True)
    e = es[0] if len(es) == 1 else jnp.concatenate(es, axis=0)
    return _dot(vt, e.astype(BF16)) * (1.0 / l)


def _half_masked(q, upper):
    lane = lax.broadcasted_iota(I32, q.shape, 1)
    keep = (lane >= 64) if upper else (lane < 64)
    return jnp.where(keep, q, jnp.zeros_like(q))


def _da_block(qm, k_t, va, m_old, acc_old):
    s = _dot(qm, k_t)
    nl = s.shape[1] // 128
    mx = s[:, 0:128]
    for c in range(1, nl):
        mx = jnp.maximum(mx, s[:, c * 128:(c + 1) * 128])
    m_new = jnp.broadcast_to(mx.max(axis=1, keepdims=True), mx.shape)
    if m_old is not None:
        m_new = jnp.maximum(m_old, m_new)
    e = jnp.concatenate([jnp.exp(s[:, c * 128:(c + 1) * 128] - m_new) for c in range(nl)], axis=1)
    pv = _dot(e.astype(BF16), va)
    if m_old is None:
        return m_new, pv
    alpha = jnp.exp(m_old - m_new)
    return m_new, acc_old * jnp.concatenate([alpha, alpha], axis=1) + pv


def _da_kernel(*refs, hp, s_new, s_cache, kb, lam_init):
    q_ref, k_ref, v_ref = refs[:3]
    pos = 3
    if s_cache:
        ck_ref, cv_ref = refs[pos:pos + 2]
        pos += 2
    lam_ref, sub_ref, o_ref, kt_scr, va_scr, m_scr, acc_scr = refs[pos:pos + 7]
    qi = pl.program_id(2)
    n_new, n_all = s_new // kb, (s_new + s_cache) // kb

    @pl.when(qi == 0)
    def _():
        for hh in range(hp):
            sl = slice(hh * 128, (hh + 1) * 128)
            for j in range(n_new):
                kt_scr[hh, j] = k_ref[j * kb:(j + 1) * kb, sl].astype(F32).T.astype(BF16)
            va_scr[hh, 0:s_new, 0:128] = v_ref[:, sl]
            for j in range(n_all - n_new):
                kt_scr[hh, n_new + j] = ck_ref[j * kb:(j + 1) * kb, sl].T.astype(BF16)
            if s_cache:
                va_scr[hh, s_new:s_new + s_cache, 0:128] = cv_ref[:, sl].astype(BF16)
            va_scr[hh, :, 128:256] = jnp.ones((s_new + s_cache, 128), BF16)

    lv = lam_ref[...]
    lam = (jnp.exp(jnp.sum(lv[0:1] * lv[1:2], axis=1, keepdims=True))
           - jnp.exp(jnp.sum(lv[2:3] * lv[3:4], axis=1, keepdims=True)) + lam_init)
    for hh in range(hp):
        sl = slice(hh * 128, (hh + 1) * 128)
        q = q_ref[:, sl]
        qms = (_half_masked(q, False), _half_masked(q, True))
        if n_all == 1:
            accs = [_da_block(qm, kt_scr[hh, 0], va_scr[hh], None, None)[1] for qm in qms]
        else:
            m_scr[...] = jnp.full(m_scr.shape, NEG, F32)
            acc_scr[...] = jnp.zeros_like(acc_scr)

            def body(j, carry, hh=hh, qms=qms):
                va = va_scr[hh, pl.ds(pl.multiple_of(j * kb, kb), kb), :]
                k_t = kt_scr[hh, j]
                new = [_da_block(qm, k_t, va, m_scr[mp], acc_scr[mp]) for mp, qm in enumerate(qms)]
                for mp, (m_new, acc) in enumerate(new):
                    m_scr[mp] = m_new
                    acc_scr[mp] = acc
                return carry

            lax.fori_loop(0, n_all, body, 0, unroll=True)
            accs = [acc_scr[0], acc_scr[1]]
        o1 = accs[0][:, :128] / accs[0][:, 128:]
        o2 = accs[1][:, :128] / accs[1][:, 128:]
        o = o1 - lam * o2
        ms = jnp.mean(o * o, axis=1, keepdims=True)
        on = o * lax.rsqrt(ms + EPS) * sub_ref[...] * (1.0 - lam_init)
        o_ref[:, sl] = on.astype(BF16)


def _diff_attention(a, cache_k, cache_v, l, da_lam_l, subln_row, lam_init, *, prompt):
    m = a.shape[0]
    if prompt:
        hp, tq, s_new, s_cache, kb = DA_HEADS, SEQ, SEQ, 0, SEQ
        grid = (BATCH, 1, 1)
        in_specs = [
            pl.BlockSpec((SEQ, 512), lambda b, h, qi: (b, 0)),
            pl.BlockSpec((SEQ, 512), lambda b, h, qi: (b, 1)),
            pl.BlockSpec((SEQ, 512), lambda b, h, qi: (b, 2)),
        ]
        args = [a, a, a]
        out_spec = pl.BlockSpec((SEQ, 512), lambda b, h, qi: (b, 0))
    else:
        hp, tq, s_new, s_cache, kb = 1, 1024, DEC_SEQ, PAST_LEN, 512
        nq = DEC_SEQ // tq
        grid = (DEC_BATCH, DA_HEADS, nq)
        in_specs = [
            pl.BlockSpec((tq, 128), lambda b, h, qi: (b * nq + qi, h)),
            pl.BlockSpec((DEC_SEQ, 128), lambda b, h, qi: (b, 4 + h)),
            pl.BlockSpec((DEC_SEQ, 128), lambda b, h, qi: (b, 8 + h)),
            pl.BlockSpec((None, None, PAST_LEN, 128), lambda b, h, qi: (b, l, 0, h)),
            pl.BlockSpec((None, None, PAST_LEN, 128), lambda b, h, qi: (b, l, 0, h)),
        ]
        args = [a, a, a, cache_k, cache_v]
        out_spec = pl.BlockSpec((tq, 128), lambda b, h, qi: (b * nq + qi, h))
    in_specs += [
        pl.BlockSpec((4, DA_QK), lambda b, h, qi: (0, 0)),
        pl.BlockSpec((1, 128), lambda b, h, qi: (0, 0)),
    ]
    args += [da_lam_l, subln_row]
    s_all = s_new + s_cache
    return pl.pallas_call(
        functools.partial(_da_kernel, hp=hp, s_new=s_new, s_cache=s_cache, kb=kb, lam_init=lam_init),
        grid=grid,
        in_specs=in_specs,
        out_specs=out_spec,
        out_shape=jax.ShapeDtypeStruct((m, 512), BF16),
        scratch_shapes=[pltpu.VMEM((hp, s_all // kb, 128, kb), BF16), pltpu.VMEM((hp, s_all, 256), BF16),
                        pltpu.VMEM((2, tq, 128), F32), pltpu.VMEM((2, tq, 256), F32)],
        compiler_params=_cparams(("arbitrary", "arbitrary", "arbitrary"), 48),
        name="diff_attn_ctx" if prompt else "diff_attn_lat",
    )(*args)


def _na_bias_kernel(rpb_ref, o_ref):
    h = pl.program_id(0)
    n_dc = 2 * NA_WIN_COLS - 1
    n_dr = 2 * NA_WIN_ROWS - 1
    kc = lax.broadcasted_iota(I32, (GRID_W, 128), 0)
    lane = lax.broadcasted_iota(I32, (GRID_W, 128), 1)
    qc = lane & (GRID_W - 1)
    dcm = kc - qc + (NA_WIN_COLS - 1)
    cstart = jnp.clip(qc - NA_WIN_COLS // 2, 0, GRID_W - NA_WIN_COLS)
    left = lane < GRID_W
    base = h * (n_dr * n_dc)
    for d in range(16):
        acc = jnp.zeros((GRID_W, 128), F32)
        for dc in range(n_dc):
            lv = rpb_ref[base + d * n_dc + dc] if d < n_dr else 0.0
            rv = rpb_ref[base + (d - 1) * n_dc + dc] if d >= 1 else 0.0
            acc = jnp.where(dcm == dc, jnp.where(left, lv, rv), acc)
        o_ref[0, d] = jnp.where(kc >= cstart, jnp.where(kc < cstart + NA_WIN_COLS, acc, NEG), NEG)


def _na_bias_table(rpb_l):
    return pl.pallas_call(
        _na_bias_kernel,
        grid=(NA_HEADS,),
        in_specs=[pl.BlockSpec(memory_space=pltpu.SMEM)],
        out_specs=pl.BlockSpec((1, 16, GRID_W, 128), lambda h: (h, 0, 0, 0)),
        out_shape=jax.ShapeDtypeStruct((NA_HEADS, 16, GRID_W, 128), F32),
        compiler_params=_cparams(("arbitrary",), 16),
        name="na_bias_table",
    )(rpb_l.reshape(-1))


def _na_lat_kernel(q_ref, k_ref, v_ref, ck_ref, cv_ref, bp_ref, o_ref):
    rb = pl.program_id(1)
    ws = jnp.clip(2 * rb - 1, 0, 12)
    delta = 4 * ws - 8 * rb
    tok0 = pl.multiple_of(ws * 256, 256)
    kr = 4 * ws + (lax.broadcasted_iota(I32, (1024, 512), 0) >> 6)
    qr = 8 * rb + (lax.broadcasted_iota(I32, (1024, 512), 1) >> 6)
    st = jnp.clip(qr - NA_WIN_ROWS // 2, 0, GRID_W - NA_WIN_ROWS)
    rowmask = jnp.where(kr >= st, jnp.where(kr < st + NA_WIN_ROWS, 0.0, NEG), NEG)
    for g in range(NA_HEADS // 2):
        sl = slice(g * 128, (g + 1) * 128)
        q2 = q_ref[:, sl]
        kall = jnp.concatenate([k_ref[pl.ds(tok0, 1024), sl], ck_ref[:, sl].astype(BF16)], axis=0)
        vall = jnp.concatenate([v_ref[pl.ds(tok0, 1024), sl].astype(F32), cv_ref[:, sl]], axis=0)
        vt = vall.T.astype(BF16)
        outs = []
        for par in range(2):
            h = 2 * g + par
            st_all = _dot_nt(kall, _half_masked(q2, par == 1))
            rows = []
            for krl in range(16):
                tiles = []
                for jq in range(4):
                    d = jnp.clip(krl - 2 * jq + 7 + delta, 0, 15)
                    tiles.append(bp_ref[h, d])
                rows.append(jnp.concatenate(tiles, axis=1))
            bias = jnp.concatenate(rows, axis=0)
            s_loc = st_all[:1024] + bias + rowmask
            outs.append(_softmax_pv([s_loc, st_all[1024:]], vt))
        o_t = jnp.concatenate([outs[0][:64], outs[1][64:]], axis=0)
        o_ref[:, sl] = o_t.T.astype(BF16)


def _na_lat(a, cache_k, cache_v, l, bias_tab):
    return pl.pallas_call(
        _na_lat_kernel,
        grid=(DEC_BATCH, 8),
        in_specs=[
            pl.BlockSpec((512, 512), lambda b, rb: (b * 8 + rb, 7)),
            pl.BlockSpec((DEC_SEQ, 512), lambda b, rb: (b, 8)),
            pl.BlockSpec((DEC_SEQ, 512), lambda b, rb: (b, 9)),
            pl.BlockSpec((None, None, PAST_LEN, 512), lambda b, rb: (b, l, 0, 0)),
            pl.BlockSpec((None, None, PAST_LEN, 512), lambda b, rb: (b, l, 0, 0)),
            pl.BlockSpec((NA_HEADS, 16, GRID_W, 128), lambda b, rb: (0, 0, 0, 0)),
        ],
        out_specs=pl.BlockSpec((512, 512), lambda b, rb: (b * 8 + rb, 0)),
        out_shape=jax.ShapeDtypeStruct((DEC_BATCH * DEC_SEQ, 512), BF16),
        compiler_params=_cparams(("arbitrary", "arbitrary"), 56),
        name="nbr_attn_lat",
    )(a, a, a, cache_k, cache_v, bias_tab)


def _na_ctx_kernel(q_ref, k_ref, v_ref, o_ref):
    for g in range(NA_HEADS // 2):
        sl = slice(g * 128, (g + 1) * 128)
        q2 = q_ref[:, sl]
        k2 = k_ref[:, sl]
        vt = v_ref[:, sl].astype(F32).T.astype(BF16)
        o_a = _softmax_pv([_dot_nt(k2, _half_masked(q2, False))], vt)
        o_b = _softmax_pv([_dot_nt(k2, _half_masked(q2, True))], vt)
        o_t = jnp.concatenate([o_a[:64], o_b[64:]], axis=0)
        o_ref[:, sl] = o_t.T.astype(BF16)


def _na_ctx(a):
    return pl.pallas_call(
        _na_ctx_kernel,
        grid=(BATCH,),
        in_specs=[
            pl.BlockSpec((SEQ, 512), lambda b: (b, 7)),
            pl.BlockSpec((SEQ, 512), lambda b: (b, 8)),
            pl.BlockSpec((SEQ, 512), lambda b: (b, 9)),
        ],
        out_specs=pl.BlockSpec((SEQ, 512), lambda b: (b, 0)),
        out_shape=jax.ShapeDtypeStruct((BATCH * SEQ, 512), BF16),
        compiler_params=_cparams(("arbitrary",), 32),
        name="nbr_attn_ctx",
    )(a, a, a)


def _log_sigmoid(x):
    return jnp.minimum(x, 0.0) - jnp.log1p(jnp.exp(-jnp.abs(x)))


def _split3(x):
    x1 = x.astype(BF16)
    r1 = x - x1.astype(F32)
    x2 = r1.astype(BF16)
    x3 = (r1 - x2.astype(F32)).astype(BF16)
    return x1, x2, x3


def _ml_kernel(*refs, t_len, nb, unroll, zero_init, emit_state):
    q_ref, k_ref, v_ref, og_ref, g_ref = refs[:5]
    pos = 5
    if not zero_init:
        c0_ref, n0_ref, m0_ref = refs[pos:pos + 3]
        pos += 3
    nrm_ref, o_ref = refs[pos:pos + 2]
    pos += 2
    if emit_state:
        cst_ref, nst_ref, mst_ref = refs[pos:pos + 3]
        pos += 3
    hsum_scr, colb_scr, rowt_scr, s_scr, m_scr = refs[pos:pos + 5]
    step = pl.program_id(0)
    lc = ML_LC
    nc = t_len // lc
    chains = [(bi, h, d) for bi in range(nb) for h in range(ML_HEADS) for d in range(2)]

    ri = lax.broadcasted_iota(I32, (lc, lc), 0)
    ci = lax.broadcasted_iota(I32, (lc, lc), 1)
    lo = ri >= ci
    up = ri <= ci
    lo_b = jnp.where(lo, 1.0, 0.0).astype(BF16)
    up_b = jnp.where(up, 1.0, 0.0).astype(BF16)
    lane = lax.broadcasted_iota(I32, (lc, 128), 1)
    is_forget = (lane & 1) == 1
    is_bwd = (lane & 2) == 2

    def tri_left(tri, x):
        return sum(_dot(tri, p) for p in _split3(x))

    def prep(c, carry):
        r0 = pl.multiple_of(c * lc, lc)
        g = g_ref[pl.ds(r0, lc), :]
        lf = _log_sigmoid(g)
        cb = jnp.where(is_forget, jnp.where(is_bwd, tri_left(up_b, lf), tri_left(lo_b, lf)), g)
        colb_scr[pl.ds(r0, lc), :] = cb
        rowt_scr[c] = (pltpu.roll(cb, 127, 1) - cb).T[0:16]
        return carry

    lax.fori_loop(0, nb * nc, prep, 0, unroll=2)

    hsum_scr[...] = jnp.zeros_like(hsum_scr)
    if zero_init:
        s_scr[...] = jnp.zeros_like(s_scr)
        m_scr[...] = jnp.zeros_like(m_scr)
    else:
        for idx, (bi, h, d) in enumerate(chains):
            n_rep = jnp.broadcast_to(n0_ref[d, h:h + 1, :], (ML_DIM, ML_DIM)).T
            s_scr[idx * ML_DIM:(idx + 1) * ML_DIM, :] = jnp.concatenate([c0_ref[d, h], n_rep], axis=1)
            m0 = m0_ref[((step * nb + bi) * 2 + d) * ML_HEADS + h]
            m_scr[idx:idx + 1, :] = jnp.full((1, 128), m0, F32)

    ones_b = jnp.ones((lc, ML_DIM), BF16)
    n_ch = len(chains)

    def body(i, carry):
        cs = [i if d == 0 else nc - 1 - i for (_, _, d) in chains]
        r0s = [pl.multiple_of(bi * t_len + c * lc, lc) for (bi, _, _), c in zip(chains, cs)]
        hsl = [slice(h * 128, (h + 1) * 128) for (_, h, _) in chains]
        qs = [q_ref[pl.ds(r0, lc), sl] for r0, sl in zip(r0s, hsl)]
        ks = [k_ref[pl.ds(r0, lc), sl] for r0, sl in zip(r0s, hsl)]
        vs = [v_ref[pl.ds(r0, lc), sl] for r0, sl in zip(r0s, hsl)]
        s_old = [s_scr[idx * ML_DIM:(idx + 1) * ML_DIM, :] for idx in range(n_ch)]
        qk = [_dot_nt(q, k) for q, k in zip(qs, ks)]

        lhs, mts, bcs, ics, mms = [], [], [], [], []
        for idx, (bi, h, d) in enumerate(chains):
            l0 = h * 4 + d * 2
            cb = colb_scr[pl.ds(r0s[idx], lc), :]
            ic = cb[:, l0:l0 + 1]
            bc = cb[:, l0 + 1:l0 + 2]
            arow = rowt_scr[bi * nc + cs[idx], l0:l0 + 1, :]
            mm = m_scr[idx:idx + 1, 0:1]
            logd = jnp.where(lo if d == 0 else up, bc - arow, -jnp.inf)
            mt = jnp.maximum(bc + mm, logd.max(axis=1, keepdims=True))
            sc = qk[idx] * jnp.exp(logd - mt)
            inter = jnp.exp(bc + mm - mt)
            lhs.append(jnp.concatenate([(inter * qs[idx].astype(F32)).astype(BF16), sc.astype(BF16)], axis=1))
            mts.append(mt)
            bcs.append(bc)
            ics.append(ic)
            mms.append(mm)

        nds = [_dot(a, jnp.concatenate([so.astype(BF16), jnp.concatenate([v, ones_b], axis=1)], axis=0))
               for a, so, v in zip(lhs, s_old, vs)]

        hcs, wvs, decays, m_news = [], [], [], []
        for idx, (bi, h, d) in enumerate(chains):
            last = lc - 1 if d == 0 else 0
            nd = nds[idx]
            hcs.append(nd[:, :ML_DIM] / jnp.maximum(jnp.abs(nd[:, ML_DIM:]), jnp.exp(-mts[idx])))
            m_new = mts[idx][last:last + 1, :]
            bl = bcs[idx][last:last + 1, :]
            w = jnp.exp(bl - bcs[idx] + ics[idx] - m_new)
            decays.append(jnp.exp(bl + mms[idx] - m_new))
            m_news.append(jnp.broadcast_to(m_new, (1, 128)))
            wvs.append(jnp.concatenate([(w * vs[idx].astype(F32)).astype(BF16),
                                        jnp.broadcast_to(w, (lc, ML_DIM)).astype(BF16)], axis=1))
        kts = [k.astype(F32).T.astype(BF16) for k in ks]
        kv = [_dot(kt, wv) for kt, wv in zip(kts, wvs)]
        s_scr[...] = jnp.concatenate([dc * so + x for dc, so, x in zip(decays, s_old, kv)], axis=0)
        m_scr[...] = jnp.concatenate(m_news, axis=0)
        for bi in range(nb):
            for d in range(2):
                sel = [idx for idx, ch in enumerate(chains) if ch[0] == bi and ch[2] == d]
                hsum_scr[pl.ds(r0s[sel[0]], lc), :] += jnp.concatenate([hcs[idx] for idx in sel], axis=1)
        return carry

    lax.fori_loop(0, nc, body, 0, unroll=unroll)

    for bi in range(nb):
        rows = slice(bi * t_len, (bi + 1) * t_len)
        for h in range(ML_HEADS):
            sl = slice(h * 128, (h + 1) * 128)
            hs = hsum_scr[rows, sl]
            ms = jnp.mean(hs * hs, axis=-1, keepdims=True)
            y = hs * lax.rsqrt(ms + EPS) * nrm_ref[:, sl]
            o_ref[rows, sl] = (y * jax.nn.sigmoid(og_ref[rows, sl].astype(F32))).astype(BF16)
    if emit_state:
        for idx, (bi, h, d) in enumerate(chains):
            st = s_scr[idx * ML_DIM:(idx + 1) * ML_DIM, :]
            cst_ref[bi, d, h] = st[:, :ML_DIM]
            nst_ref[bi, d, h:h + 1, :] = st[:, ML_DIM:].T[0:1, :]
            mst_ref[bi, d, h:h + 1, :] = m_scr[idx:idx + 1, :]


def _mlstm(a, gates, states, l, ml_norm_l, *, prompt):
    m = a.shape[0]
    t_len = SEQ if prompt else DEC_SEQ
    nb = 2 if prompt else 1
    n_seq = m // t_len
    rows = nb * t_len
    big = {} if prompt else dict(pipeline_mode=pl.Buffered(1))
    in_specs = [
        pl.BlockSpec((rows, 512), lambda i: (i, 3), **big),
        pl.BlockSpec((rows, 512), lambda i: (i, 4), **big),
        pl.BlockSpec((rows, 512), lambda i: (i, 5), **big),
        pl.BlockSpec((rows, 512), lambda i: (i, 6), **big),
        pl.BlockSpec((rows, 128), lambda i: (i, 0)),
    ]
    args = [a, a, a, a, gates]
    if not prompt:
        c0, n0, m0 = states
        in_specs += [
            pl.BlockSpec((None, None, 2, ML_HEADS, ML_DIM, ML_DIM), lambda i: (i, l, 0, 0, 0, 0)),
            pl.BlockSpec((None, None, 2, ML_HEADS, ML_DIM), lambda i: (i, l, 0, 0, 0)),
            pl.BlockSpec(memory_space=pltpu.SMEM),
        ]
        args += [c0, n0, m0[:, l].reshape(-1)]
    in_specs.append(pl.BlockSpec((1, 512), lambda i: (0, 0)))
    args.append(ml_norm_l.reshape(1, 512))
    out_shape = [jax.ShapeDtypeStruct((m, 512), BF16)]
    out_specs = [pl.BlockSpec((rows, 512), lambda i: (i, 0))]
    if prompt:
        out_shape += [
            jax.ShapeDtypeStruct((n_seq, 2, ML_HEADS, ML_DIM, ML_DIM), F32),
            jax.ShapeDtypeStruct((n_seq, 2, ML_HEADS, ML_DIM), F32),
            jax.ShapeDtypeStruct((n_seq, 2, ML_HEADS, 128), F32),
        ]
        out_specs += [
            pl.BlockSpec((nb, 2, ML_HEADS, ML_DIM, ML_DIM), lambda i: (i, 0, 0, 0, 0)),
            pl.BlockSpec((nb, 2, ML_HEADS, ML_DIM), lambda i: (i, 0, 0, 0)),
            pl.BlockSpec((nb, 2, ML_HEADS, 128), lambda i: (i, 0, 0, 0)),
        ]
    n_chain = nb * ML_HEADS * 2
    return pl.pallas_call(
        functools.partial(_ml_kernel, t_len=t_len, nb=nb, unroll=2 if prompt else 1,
                          zero_init=prompt, emit_state=prompt),
        grid=(n_seq // nb,),
        in_specs=in_specs,
        out_specs=out_specs,
        out_shape=out_shape,
        scratch_shapes=[pltpu.VMEM((rows, 512), F32), pltpu.VMEM((rows, 128), F32),
                        pltpu.VMEM((rows // ML_LC, 16, ML_LC), F32),
                        pltpu.VMEM((n_chain * ML_DIM, 2 * ML_DIM), F32), pltpu.VMEM((n_chain, 128), F32)],
        compiler_params=_cparams(("arbitrary",), 56),
        name="mlstm_ctx" if prompt else "mlstm_lat",
    )(*args)


def _merge_kernel(x_ref, mod_ref, oda_ref, oml_ref, ona_ref, g0_ref, g1_ref, g2_ref,
                  wda_ref, wml_ref, wna_ref, wo_ref, o_ref):
    def branch(o, w, g):
        return jax.nn.sigmoid(g[...].astype(F32)) * _dot(o[...], w[...])

    merged = (branch(oda_ref, wda_ref, g0_ref) + branch(oml_ref, wml_ref, g1_ref)
              + branch(ona_ref, wna_ref, g2_ref))
    o_ref[...] = x_ref[...] + mod_ref[2:3, :] * _dot(merged.astype(BF16), wo_ref[...])


def _mod_row_map(prompt, tm):
    if prompt:
        return lambda i, *_: (0, 0, 0)
    return lambda i, *_: (1 + i // (DEC_SEQ // tm), 0, 0)


def _merge_out(x, modl, o_da, o_ml, o_na, a, w_da, w_ml, w_na, w_out, *, prompt):
    m = x.shape[0]
    tm = 512
    const = lambda i: (0, 0)
    return pl.pallas_call(
        _merge_kernel,
        grid=(m // tm,),
        in_specs=[
            pl.BlockSpec((tm, D_MODEL), lambda i: (i, 0)),
            pl.BlockSpec((None, 6, D_MODEL), _mod_row_map(prompt, tm)),
            pl.BlockSpec((tm, 512), lambda i: (i, 0)),
            pl.BlockSpec((tm, 512), lambda i: (i, 0)),
            pl.BlockSpec((tm, 512), lambda i: (i, 0)),
            pl.BlockSpec((tm, D_MODEL), lambda i: (i, 5)),
            pl.BlockSpec((tm, D_MODEL), lambda i: (i, 6)),
            pl.BlockSpec((tm, D_MODEL), lambda i: (i, 7)),
            pl.BlockSpec((512, D_MODEL), const),
            pl.BlockSpec((512, D_MODEL), const),
            pl.BlockSpec((512, D_MODEL), const),
            pl.BlockSpec((D_MODEL, D_MODEL), const),
        ],
        out_specs=pl.BlockSpec((tm, D_MODEL), lambda i: (i, 0)),
        out_shape=jax.ShapeDtypeStruct((m, D_MODEL), F32),
        compiler_params=_cparams(("arbitrary",), 48),
        name="merge_out_ctx" if prompt else "merge_out_lat",
    )(x, modl, o_da, o_ml, o_na, a, a, a, w_da, w_ml, w_na, w_out)


def _ffn_kernel(x_ref, mod_ref, g_ref, w1_ref, b1_ref, w2_ref, b2_ref, o_ref, h_scr, acc_scr):
    f = pl.program_id(1)

    @pl.when(f == 0)
    def _():
        h_scr[...] = _modulated_norm(x_ref[...], g_ref[...], mod_ref[3:4, :], mod_ref[4:5, :]).astype(BF16)
        acc_scr[...] = jnp.zeros_like(acc_scr)

    u = jnp.maximum(_dot(h_scr[...], w1_ref[...]) + b1_ref[...], 0.0)
    acc_scr[...] += _dot((u * u).astype(BF16), w2_ref[...])

    @pl.when(f == pl.num_programs(1) - 1)
    def _():
        o_ref[...] = x_ref[...] + mod_ref[5:6, :] * (acc_scr[...] + b2_ref[...])


def _ffn(x, modl, g2, w1, b1, w2, b2, *, prompt):
    m = x.shape[0]
    tm, tf = 1024, 512
    return pl.pallas_call(
        _ffn_kernel,
        grid=(m // tm, D_FF // tf),
        in_specs=[
            pl.BlockSpec((tm, D_MODEL), lambda i, f: (i, 0)),
            pl.BlockSpec((None, 6, D_MODEL), _mod_row_map(prompt, tm)),
            pl.BlockSpec((1, D_MODEL), lambda i, f: (0, 0)),
            pl.BlockSpec((D_MODEL, tf), lambda i, f: (0, f)),
            pl.BlockSpec((1, tf), lambda i, f: (0, f)),
            pl.BlockSpec((tf, D_MODEL), lambda i, f: (f, 0)),
            pl.BlockSpec((1, D_MODEL), lambda i, f: (0, 0)),
        ],
        out_specs=pl.BlockSpec((tm, D_MODEL), lambda i, f: (i, 0)),
        out_shape=jax.ShapeDtypeStruct((m, D_MODEL), F32),
        scratch_shapes=[pltpu.VMEM((tm, D_MODEL), BF16), pltpu.VMEM((tm, D_MODEL), F32)],
        compiler_params=_cparams(("arbitrary", "arbitrary"), 48),
        name="ffn_ctx" if prompt else "ffn_lat",
    )(x, modl, g2, w1, b1, w2, b2)


def _final_norm_kernel(x_ref, g_ref, o_ref):
    x = x_ref[...]
    ms = jnp.mean(x * x, axis=-1, keepdims=True)
    o_ref[...] = x * lax.rsqrt(ms + EPS) * g_ref[...]


def _final_norm(x, g):
    m = x.shape[0]
    tm = 1024
    return pl.pallas_call(
        _final_norm_kernel,
        grid=(m // tm,),
        in_specs=[pl.BlockSpec((tm, D_MODEL), lambda i: (i, 0)),
                  pl.BlockSpec((1, D_MODEL), lambda i: (0, 0))],
        out_specs=pl.BlockSpec((tm, D_MODEL), lambda i: (i, 0)),
        out_shape=jax.ShapeDtypeStruct((m, D_MODEL), F32),
        compiler_params=_cparams(("arbitrary",), 32),
        name="final_norm",
    )(x, g)


def _rope_tables():
    t = jnp.arange(DEC_SEQ)
    lane = jnp.arange(512)
    sub = lane % 32
    freq = ROPE_BASE ** (-(2 * (sub % 16)).astype(F32) / 32.0)
    use_row = (lane % 64) < 32
    posv = jnp.where(use_row[None, :], (t // GRID_W)[:, None], (t % GRID_W)[:, None]).astype(F32)
    ang = posv * freq[None, :]
    sign = jnp.where(sub < 16, -1.0, 1.0).astype(F32)
    return jnp.cos(ang), jnp.sin(ang) * sign[None, :]


def kernel(x_prompt, x_sample, cache_da_k, cache_da_v, cache_na_k, cache_na_v, state_ml_C, state_ml_n,
           state_ml_m, c, c_ctx, w_mod, b_mod, norm1, w_in, b_in, da_lam, da_subln, ml_norm, na_rpb,
           w_up_da, w_up_ml, w_up_na, w_out, norm2, w_ff1, b_ff1, w_ff2, b_ff2, norm_f):
    xp = x_prompt.reshape(BATCH * SEQ, D_MODEL)
    xs = x_sample.reshape(DEC_BATCH * DEC_SEQ, D_MODEL)

    w_main = jnp.concatenate([w_in[..., :GATE_OFF], w_in[..., GATE_OFF + 16:]], axis=-1).astype(BF16)
    b_main = jnp.concatenate([b_in[..., :GATE_OFF], b_in[..., GATE_OFF + 16:]], axis=-1).reshape(DEPTH, 1, N_MAIN)
    src, dst = [], []
    for hh in range(ML_HEADS):
        for dr in range(2):
            for tt in range(2):
                src.append(GATE_OFF + dr * 2 * ML_HEADS + tt * ML_HEADS + hh)
                dst.append(hh * 4 + dr * 2 + tt)
    src, dst = np.array(src), np.array(dst)
    w_g = jnp.zeros((DEPTH, D_MODEL, 128), F32).at[:, :, dst].set(w_in[:, :, src]).astype(BF16)
    b_g = jnp.zeros((DEPTH, 1, 128), F32).at[:, 0, dst].set(b_in[:, src])
    cs = np.ones((1, N_MAIN), np.float32)
    cs[0, 0:512] = DA_QK ** -0.5
    cs[0, 2048:2560] = ML_DIM ** -0.5
    cs[0, 3584:4096] = NA_DIM ** -0.5
    cscale = jnp.asarray(cs)
    w_da_b, w_ml_b, w_na_b, w_out_b = (w.astype(BF16) for w in (w_up_da, w_up_ml, w_up_na, w_out))
    w_ff1_b, w_ff2_b = w_ff1.astype(BF16), w_ff2.astype(BF16)
    rope_tabs = _rope_tables()

    cc = jnp.zeros((8, D_MODEL), F32).at[0].set(c_ctx).at[1:1 + DEC_BATCH].set(c)
    mod = _modulation(cc, w_mod, b_mod).reshape(DEPTH, 8, 6, D_MODEL)

    cdk = cache_da_k.reshape(DEC_BATCH, DEPTH, PAST_LEN, 512)
    cdv = cache_da_v.reshape(DEC_BATCH, DEPTH, PAST_LEN, 512)
    cnk = cache_na_k.reshape(DEC_BATCH, DEPTH, PAST_LEN, 512)
    cnv = cache_na_v.reshape(DEC_BATCH, DEPTH, PAST_LEN, 512)

    coll = [[] for _ in range(7)]
    for l in range(DEPTH):
        lam_init = 0.8 - 0.6 * math.exp(-0.3 * l)
        modl = mod[l]
        g1 = norm1[l].reshape(1, D_MODEL)
        g2 = norm2[l].reshape(1, D_MODEL)
        sub_row = da_subln[l].reshape(1, 128)
        bias_tab = _na_bias_table(na_rpb[l])
        for prompt in (True, False):
            x = xp if prompt else xs
            outs = _in_proj(x, modl, g1, w_main[l], b_main[l], cscale, w_g[l], b_g[l], rope_tabs, prompt=prompt)
            a, gates = outs[0], outs[1]
            if prompt:
                for lst, arr in zip(coll[:4], outs[2:6]):
                    lst.append(arr)
            o_da = _diff_attention(a, cdk, cdv, l, da_lam[l], sub_row, lam_init, prompt=prompt)
            ml_out = _mlstm(a, gates, (state_ml_C, state_ml_n, state_ml_m), l, ml_norm[l], prompt=prompt)
            o_ml = ml_out[0]
            if prompt:
                coll[4].append(ml_out[1])
                coll[5].append(ml_out[2])
                coll[6].append(ml_out[3][..., 0])
                o_na = _na_ctx(a)
            else:
                o_na = _na_lat(a, cnk, cnv, l, bias_tab)
            x = _merge_out(x, modl, o_da, o_ml, o_na, a, w_da_b[l], w_ml_b[l], w_na_b[l], w_out_b[l],
                           prompt=prompt)
            x = _ffn(x, modl, g2, w_ff1_b[l], b_ff1[l].reshape(1, D_FF), w_ff2_b[l],
                     b_ff2[l].reshape(1, D_MODEL), prompt=prompt)
            if prompt:
                xp = x
            else:
                xs = x

    gf = norm_f.reshape(1, D_MODEL)
    y_prompt = _final_norm(xp, gf).reshape(BATCH, SEQ, D_MODEL)
    y_sample = _final_norm(xs, gf).reshape(DEC_BATCH, DEC_SEQ, D_MODEL)
    new_da_k = jnp.stack(coll[0], axis=1).reshape(BATCH, DEPTH, SEQ, DA_HEADS, 2 * DA_QK)
    new_da_v = jnp.stack(coll[1], axis=1).reshape(BATCH, DEPTH, SEQ, DA_HEADS, 2 * DA_QK)
    new_na_k = jnp.stack(coll[2], axis=1).reshape(BATCH, DEPTH, SEQ, NA_HEADS, NA_DIM)
    new_na_v = jnp.stack(coll[3], axis=1).reshape(BATCH, DEPTH, SEQ, NA_HEADS, NA_DIM)
    new_ml_c = jnp.stack(coll[4], axis=1)
    new_ml_n = jnp.stack(coll[5], axis=1)
    new_ml_m = jnp.stack(coll[6], axis=1)
    return (y_prompt, y_sample, new_da_k, new_da_v, new_na_k, new_na_v, new_ml_c, new_ml_n, new_ml_m)
```

```python
import functools
import math

import jax
import jax.numpy as jnp
import numpy as np
from jax import lax
from jax.experimental import pallas as pl
from jax.experimental.pallas import tpu as pltpu

F32 = jnp.float32
BF16 = jnp.bfloat16
I32 = jnp.int32

D_MODEL = 1024
BATCH = 32
SEQ = 256
DEPTH = 4
DEC_BATCH = 2
DEC_SEQ = 4096
PAST_LEN = 512
GRID_W = 64
DA_HEADS = 4
DA_QK = 64
ML_HEADS = 4
ML_DIM = 128
ML_LC = 128
NA_HEADS = 8
NA_DIM = 64
NA_WIN_ROWS = 8
NA_WIN_COLS = 16
D_FF = 4 * D_MODEL
ROPE_BASE = 10000.0
EPS = 1e-6
N_MAIN = 8192
GATE_OFF = 3584
NEG = -1e30

MIB = 1024 * 1024
NT_DIMS = (((1,), (1,)), ((), ()))


def _cparams(sem, vmem_mib):
    return pltpu.CompilerParams(dimension_semantics=sem, vmem_limit_bytes=vmem_mib * MIB)


def _dot(a, b):
    return jnp.dot(a, b, preferred_element_type=F32)


def _dot_nt(a, b):
    return lax.dot_general(a, b, NT_DIMS, preferred_element_type=F32)


def _mod_kernel(c_ref, w_ref, b_ref, o_ref):
    c = c_ref[...]
    s = (c * jax.nn.sigmoid(c)).astype(BF16)
    o_ref[0] = _dot(s, w_ref[0].astype(BF16)) + b_ref[0]


def _modulation(cc, w_mod, b_mod):
    tn = 1536
    n = 6 * D_MODEL
    return pl.pallas_call(
        _mod_kernel,
        grid=(DEPTH, n // tn),
        in_specs=[
            pl.BlockSpec((8, D_MODEL), lambda l, j: (0, 0)),
            pl.BlockSpec((1, D_MODEL, tn), lambda l, j: (l, 0, j)),
            pl.BlockSpec((1, 1, tn), lambda l, j: (l, 0, j)),
        ],
        out_specs=pl.BlockSpec((1, 8, tn), lambda l, j: (l, 0, j)),
        out_shape=jax.ShapeDtypeStruct((DEPTH, 8, n), F32),
        compiler_params=_cparams(("arbitrary", "arbitrary"), 40),
        name="modulation",
    )(cc, w_mod, b_mod.reshape(DEPTH, 1, n))


def _modulated_norm(x, g, shift, scale):
    ms = jnp.mean(x * x, axis=-1, keepdims=True)
    y = x * lax.rsqrt(ms + EPS) * g
    return y * (1.0 + scale) + shift


IP_TM = 1024
IP_SUB = 512
N_KV = 2048


def _inproj_kernel(x_ref, mod_ref, g_ref, w_ref, b_ref, cs_ref, wg_ref, bg_ref, a_ref, gate_ref, h_scr):
    @pl.when(pl.program_id(1) == 0)
    def _():
        h = _modulated_norm(x_ref[...], g_ref[...], mod_ref[0:1, :], mod_ref[1:2, :]).astype(BF16)
        h_scr[...] = h
        gate_ref[...] = _dot(h, wg_ref[...]) + bg_ref[...]

    h = h_scr[...]
    for c in range(a_ref.shape[1] // IP_SUB):
        cs = slice(c * IP_SUB, (c + 1) * IP_SUB)
        a_ref[:, cs] = ((_dot(h, w_ref[:, cs]) + b_ref[:, cs]) * cs_ref[:, cs]).astype(BF16)


def _inproj_kv_kernel(x_ref, mod_ref, g_ref, w_ref, b_ref, *refs):
    a_ref = refs[4]
    f32_refs = refs[5:9]
    h = _modulated_norm(x_ref[...], g_ref[...], mod_ref[0:1, :], mod_ref[1:2, :]).astype(BF16)
    for c, ref in enumerate(f32_refs):
        cs = slice(c * IP_SUB, (c + 1) * IP_SUB)
        acc = _dot(h, w_ref[:, cs]) + b_ref[:, cs]
        ref[...] = acc.reshape(ref.shape)
        a_ref[:, cs] = acc.astype(BF16)


def _mod_row_map(prompt, tm):
    if prompt:
        return lambda i, *_: (0, 0, 0)
    return lambda i, *_: (1 + i // (DEC_SEQ // tm), 0, 0)


def _in_proj(x, modl, g1, w_main, b_main, cscale, w_g, b_g, *, col0, prompt):
    m = x.shape[0]
    tm, tn = IP_TM, 1024
    j0 = col0 // tn
    return pl.pallas_call(
        _inproj_kernel,
        grid=(m // tm, (N_MAIN - col0) // tn),
        in_specs=[
            pl.BlockSpec((tm, D_MODEL), lambda i, j: (i, 0)),
            pl.BlockSpec((None, 6, D_MODEL), _mod_row_map(prompt, tm)),
            pl.BlockSpec((1, D_MODEL), lambda i, j: (0, 0)),
            pl.BlockSpec((D_MODEL, tn), lambda i, j: (0, j + j0)),
            pl.BlockSpec((1, tn), lambda i, j: (0, j + j0)),
            pl.BlockSpec((1, tn), lambda i, j: (0, j + j0)),
            pl.BlockSpec((D_MODEL, 128), lambda i, j: (0, 0)),
            pl.BlockSpec((1, 128), lambda i, j: (0, 0)),
        ],
        out_specs=[pl.BlockSpec((tm, tn), lambda i, j: (i, j)),
                   pl.BlockSpec((tm, 128), lambda i, j: (i, 0))],
        out_shape=[jax.ShapeDtypeStruct((m, N_MAIN - col0), BF16), jax.ShapeDtypeStruct((m, 128), F32)],
        scratch_shapes=[pltpu.VMEM((tm, D_MODEL), BF16)],
        compiler_params=_cparams(("arbitrary", "arbitrary"), 48),
        name="in_proj_ctx" if prompt else "in_proj_lat",
    )(x, modl, g1, w_main, b_main, cscale, w_g, b_g)


def _in_proj_kv(x, modl, g1, w_main, b_main, kv_outs, l):
    m = x.shape[0]
    tm = IP_TM
    nbat = tm // SEQ
    kv_spec = pl.BlockSpec((nbat, None, SEQ, IP_SUB), lambda i: (i, l, 0, 0))
    outs = pl.pallas_call(
        _inproj_kv_kernel,
        grid=(m // tm,),
        in_specs=[
            pl.BlockSpec((tm, D_MODEL), lambda i: (i, 0)),
            pl.BlockSpec((None, 6, D_MODEL), _mod_row_map(True, tm)),
            pl.BlockSpec((1, D_MODEL), lambda i: (0, 0)),
            pl.BlockSpec((D_MODEL, N_KV), lambda i: (0, 0)),
            pl.BlockSpec((1, N_KV), lambda i: (0, 0)),
        ] + [pl.BlockSpec(memory_space=pl.ANY)] * 4,
        out_specs=[pl.BlockSpec((tm, N_KV), lambda i: (i, 0))] + [kv_spec] * 4,
        out_shape=[jax.ShapeDtypeStruct((m, N_KV), BF16)]
                  + [jax.ShapeDtypeStruct(o.shape, o.dtype) for o in kv_outs],
        input_output_aliases={5 + c: 1 + c for c in range(4)},
        compiler_params=_cparams(("arbitrary",), 56),
        name="in_proj_kv_ctx",
    )(x, modl, g1, w_main, b_main, *kv_outs)
    return outs[0], outs[1:]


def _softmax_pv(s_parts, vt):
    m = s_parts[0].max(axis=0, keepdims=True)
    for s in s_parts[1:]:
        m = jnp.maximum(m, s.max(axis=0, keepdims=True))
    es = [jnp.exp(s - m) for s in s_parts]
    l = es[0].sum(axis=0, keepdims=True)
    for e in es[1:]:
        l = l + e.sum(axis=0, keepdims=True)
    e = es[0] if len(es) == 1 else jnp.concatenate(es, axis=0)
    return _dot(vt, e.astype(BF16)) * (1.0 / l)


def _half_masked(q, upper):
    lane = lax.broadcasted_iota(I32, q.shape, 1)
    keep = (lane >= 64) if upper else (lane < 64)
    return jnp.where(keep, q, jnp.zeros_like(q))


def _da_block(qm, k_t, va, m_old, acc_old):
    s = _dot(qm, k_t)
    nl = s.shape[1] // 128
    mx = s[:, 0:128]
    for c in range(1, nl):
        mx = jnp.maximum(mx, s[:, c * 128:(c + 1) * 128])
    m_new = jnp.broadcast_to(mx.max(axis=1, keepdims=True), mx.shape)
    if m_old is not None:
        m_new = jnp.maximum(m_old, m_new)
    e = jnp.concatenate([jnp.exp(s[:, c * 128:(c + 1) * 128] - m_new) for c in range(nl)], axis=1)
    pv = _dot(e.astype(BF16), va)
    if m_old is None:
        return m_new, pv
    alpha = jnp.exp(m_old - m_new)
    return m_new, acc_old * jnp.concatenate([alpha, alpha], axis=1) + pv


def _rope(x, cos, sin):
    lane = lax.broadcasted_iota(I32, x.shape, 1)
    partner = jnp.where((lane & 16) == 0, pltpu.roll(x, 112, 1), pltpu.roll(x, 16, 1))
    return x * cos + partner * sin


def _da_kernel(*refs, hp, s_new, s_cache, kb, rope, lam_init):
    q_ref, k_ref, v_ref = refs[:3]
    pos = 3
    if s_cache:
        ck_ref, cv_ref = refs[pos:pos + 2]
        pos += 2
    if rope:
        cos_ref, sin_ref = refs[pos:pos + 2]
        pos += 2
    lam_ref, sub_ref, o_ref, kt_scr, va_scr, m_scr, acc_scr = refs[pos:pos + 7]
    qi = pl.program_id(2)
    tq = q_ref.shape[0]
    n_new, n_all = s_new // kb, (s_new + s_cache) // kb

    @pl.when(qi == 0)
    def _():
        for hh in range(hp):
            sl = slice(hh * 128, (hh + 1) * 128)
            for j in range(n_new):
                rows = slice(j * kb, (j + 1) * kb)
                kblk = k_ref[rows, sl].astype(F32)
                if rope:
                    kblk = _rope(kblk, cos_ref[rows, :], sin_ref[rows, :])
                kt_scr[hh, j] = kblk.T.astype(BF16)
            va_scr[hh, 0:s_new, 0:128] = v_ref[:, sl]
            for j in range(n_all - n_new):
                kt_scr[hh, n_new + j] = ck_ref[j * kb:(j + 1) * kb, sl].T.astype(BF16)
            if s_cache:
                va_scr[hh, s_new:s_new + s_cache, 0:128] = cv_ref[:, sl].astype(BF16)
            va_scr[hh, :, 128:256] = jnp.ones((s_new + s_cache, 128), BF16)

    lv = lam_ref[...]
    lam = (jnp.exp(jnp.sum(lv[0:1] * lv[1:2], axis=1, keepdims=True))
           - jnp.exp(jnp.sum(lv[2:3] * lv[3:4], axis=1, keepdims=True)) + lam_init)
    for hh in range(hp):
        sl = slice(hh * 128, (hh + 1) * 128)
        q = q_ref[:, sl]
        if rope:
            q_rows = pl.ds(pl.multiple_of(qi * tq, tq), tq)
            q = _rope(q.astype(F32), cos_ref[q_rows, :], sin_ref[q_rows, :]).astype(BF16)
        qms = (_half_masked(q, False), _half_masked(q, True))
        if n_all == 1:
            accs = [_da_block(qm, kt_scr[hh, 0], va_scr[hh], None, None)[1] for qm in qms]
        else:
            m_scr[...] = jnp.full(m_scr.shape, NEG, F32)
            acc_scr[...] = jnp.zeros_like(acc_scr)

            def body(j, carry, hh=hh, qms=qms):
                va = va_scr[hh, pl.ds(pl.multiple_of(j * kb, kb), kb), :]
                k_t = kt_scr[hh, j]
                new = [_da_block(qm, k_t, va, m_scr[mp], acc_scr[mp]) for mp, qm in enumerate(qms)]
                for mp, (m_new, acc) in enumerate(new):
                    m_scr[mp] = m_new
                    acc_scr[mp] = acc
                return carry

            lax.fori_loop(0, n_all, body, 0, unroll=True)
            accs = [acc_scr[0], acc_scr[1]]
        o1 = accs[0][:, :128] / accs[0][:, 128:]
        o2 = accs[1][:, :128] / accs[1][:, 128:]
        o = o1 - lam * o2
        ms = jnp.mean(o * o, axis=1, keepdims=True)
        on = o * lax.rsqrt(ms + EPS) * sub_ref[...] * (1.0 - lam_init)
        o_ref[:, sl] = on.astype(BF16)


def _diff_attention(cols, cache_k, cache_v, rope_tabs, l, da_lam_l, subln_row, lam_init, *, prompt):
    (qa, qt), (ka, kt), (va, vt) = cols["da_q"], cols["da_k"], cols["da_v"]
    m = qa.shape[0]
    if prompt:
        hp, tq, s_new, s_cache, kb = DA_HEADS, SEQ, SEQ, 0, SEQ
        grid = (BATCH, 1, 1)
        in_specs = [
            pl.BlockSpec((SEQ, 512), lambda b, h, qi: (b, qt)),
            pl.BlockSpec((SEQ, 512), lambda b, h, qi: (b, kt)),
            pl.BlockSpec((SEQ, 512), lambda b, h, qi: (b, vt)),
        ]
        args = [qa, ka, va]
        out_spec = pl.BlockSpec((SEQ, 512), lambda b, h, qi: (b, 0))
    else:
        hp, tq, s_new, s_cache, kb = 1, 1024, DEC_SEQ, PAST_LEN, 512
        nq = DEC_SEQ // tq
        grid = (DEC_BATCH, DA_HEADS, nq)
        in_specs = [
            pl.BlockSpec((tq, 128), lambda b, h, qi: (b * nq + qi, 4 * qt + h)),
            pl.BlockSpec((DEC_SEQ, 128), lambda b, h, qi: (b, 4 * kt + h)),
            pl.BlockSpec((DEC_SEQ, 128), lambda b, h, qi: (b, 4 * vt + h)),
            pl.BlockSpec((None, None, PAST_LEN, 128), lambda b, h, qi: (b, l, 0, h)),
            pl.BlockSpec((None, None, PAST_LEN, 128), lambda b, h, qi: (b, l, 0, h)),
            pl.BlockSpec((DEC_SEQ, 128), lambda b, h, qi: (0, 0)),
            pl.BlockSpec((DEC_SEQ, 128), lambda b, h, qi: (0, 0)),
        ]
        args = [qa, ka, va, cache_k, cache_v, *rope_tabs]
        out_spec = pl.BlockSpec((tq, 128), lambda b, h, qi: (b * nq + qi, h))
    in_specs += [
        pl.BlockSpec((4, DA_QK), lambda b, h, qi: (0, 0)),
        pl.BlockSpec((1, 128), lambda b, h, qi: (0, 0)),
    ]
    args += [da_lam_l, subln_row]
    s_all = s_new + s_cache
    return pl.pallas_call(
        functools.partial(_da_kernel, hp=hp, s_new=s_new, s_cache=s_cache, kb=kb, rope=not prompt,
                          lam_init=lam_init),
        grid=grid,
        in_specs=in_specs,
        out_specs=out_spec,
        out_shape=jax.ShapeDtypeStruct((m, 512), BF16),
        scratch_shapes=[pltpu.VMEM((hp, s_all // kb, 128, kb), BF16), pltpu.VMEM((hp, s_all, 256), BF16),
                        pltpu.VMEM((2, tq, 128), F32), pltpu.VMEM((2, tq, 256), F32)],
        compiler_params=_cparams(("arbitrary", "arbitrary", "arbitrary"), 48),
        name="diff_attn_ctx" if prompt else "diff_attn_lat",
    )(*args)


def _na_bias_kernel(rpb_ref, o_ref):
    h = pl.program_id(0)
    n_dc = 2 * NA_WIN_COLS - 1
    n_dr = 2 * NA_WIN_ROWS - 1
    kc = lax.broadcasted_iota(I32, (GRID_W, 128), 0)
    lane = lax.broadcasted_iota(I32, (GRID_W, 128), 1)
    qc = lane & (GRID_W - 1)
    dcm = kc - qc + (NA_WIN_COLS - 1)
    cstart = jnp.clip(qc - NA_WIN_COLS // 2, 0, GRID_W - NA_WIN_COLS)
    left = lane < GRID_W
    base = h * (n_dr * n_dc)
    for d in range(16):
        acc = jnp.zeros((GRID_W, 128), F32)
        for dc in range(n_dc):
            lv = rpb_ref[base + d * n_dc + dc] if d < n_dr else 0.0
            rv = rpb_ref[base + (d - 1) * n_dc + dc] if d >= 1 else 0.0
            acc = jnp.where(dcm == dc, jnp.where(left, lv, rv), acc)
        o_ref[0, d] = jnp.where(kc >= cstart, jnp.where(kc < cstart + NA_WIN_COLS, acc, NEG), NEG)


def _na_bias_table(rpb_l):
    return pl.pallas_call(
        _na_bias_kernel,
        grid=(NA_HEADS,),
        in_specs=[pl.BlockSpec(memory_space=pltpu.SMEM)],
        out_specs=pl.BlockSpec((1, 16, GRID_W, 128), lambda h: (h, 0, 0, 0)),
        out_shape=jax.ShapeDtypeStruct((NA_HEADS, 16, GRID_W, 128), F32),
        compiler_params=_cparams(("arbitrary",), 16),
        name="na_bias_table",
    )(rpb_l.reshape(-1))


def _na_lat_kernel(q_ref, k_ref, v_ref, ck_ref, cv_ref, bp_ref, o_ref):
    rb = pl.program_id(1)
    ws = jnp.clip(2 * rb - 1, 0, 12)
    delta = 4 * ws - 8 * rb
    tok0 = pl.multiple_of(ws * 256, 256)
    kr = 4 * ws + (lax.broadcasted_iota(I32, (1024, 512), 0) >> 6)
    qr = 8 * rb + (lax.broadcasted_iota(I32, (1024, 512), 1) >> 6)
    st = jnp.clip(qr - NA_WIN_ROWS // 2, 0, GRID_W - NA_WIN_ROWS)
    rowmask = jnp.where(kr >= st, jnp.where(kr < st + NA_WIN_ROWS, 0.0, NEG), NEG)
    for g in range(NA_HEADS // 2):
        sl = slice(g * 128, (g + 1) * 128)
        q2 = q_ref[:, sl]
        kall = jnp.concatenate([k_ref[pl.ds(tok0, 1024), sl], ck_ref[:, sl].astype(BF16)], axis=0)
        vall = jnp.concatenate([v_ref[pl.ds(tok0, 1024), sl].astype(F32), cv_ref[:, sl]], axis=0)
        vt = vall.T.astype(BF16)
        outs = []
        for par in range(2):
            h = 2 * g + par
            st_all = _dot_nt(kall, _half_masked(q2, par == 1))
            rows = []
            for krl in range(16):
                tiles = []
                for jq in range(4):
                    d = jnp.clip(krl - 2 * jq + 7 + delta, 0, 15)
                    tiles.append(bp_ref[h, d])
                rows.append(jnp.concatenate(tiles, axis=1))
            bias = jnp.concatenate(rows, axis=0)
            s_loc = st_all[:1024] + bias + rowmask
            outs.append(_softmax_pv([s_loc, st_all[1024:]], vt))
        o_t = jnp.concatenate([outs[0][:64], outs[1][64:]], axis=0)
        o_ref[:, sl] = o_t.T.astype(BF16)


def _na_lat(cols, cache_k, cache_v, l, bias_tab):
    (qa, qt), (ka, kt), (va, vt) = cols["na_q"], cols["na_k"], cols["na_v"]
    return pl.pallas_call(
        _na_lat_kernel,
        grid=(DEC_BATCH, 8),
        in_specs=[
            pl.BlockSpec((512, 512), lambda b, rb: (b * 8 + rb, qt)),
            pl.BlockSpec((DEC_SEQ, 512), lambda b, rb: (b, kt)),
            pl.BlockSpec((DEC_SEQ, 512), lambda b, rb: (b, vt)),
            pl.BlockSpec((None, None, PAST_LEN, 512), lambda b, rb: (b, l, 0, 0)),
            pl.BlockSpec((None, None, PAST_LEN, 512), lambda b, rb: (b, l, 0, 0)),
            pl.BlockSpec((NA_HEADS, 16, GRID_W, 128), lambda b, rb: (0, 0, 0, 0)),
        ],
        out_specs=pl.BlockSpec((512, 512), lambda b, rb: (b * 8 + rb, 0)),
        out_shape=jax.ShapeDtypeStruct((DEC_BATCH * DEC_SEQ, 512), BF16),
        compiler_params=_cparams(("arbitrary", "arbitrary"), 56),
        name="nbr_attn_lat",
    )(qa, ka, va, cache_k, cache_v, bias_tab)


def _na_ctx_kernel(q_ref, k_ref, v_ref, o_ref):
    ones_b = jnp.ones((q_ref.shape[0], 128), BF16)
    lane = lax.broadcasted_iota(I32, (q_ref.shape[0], 128), 1)
    for g in range(NA_HEADS // 2):
        sl = slice(g * 128, (g + 1) * 128)
        q2 = q_ref[:, sl]
        k_t = k_ref[:, sl].astype(F32).T.astype(BF16)
        va = jnp.concatenate([v_ref[:, sl], ones_b], axis=1)
        acc_a = _da_block(_half_masked(q2, False), k_t, va, None, None)[1]
        acc_b = _da_block(_half_masked(q2, True), k_t, va, None, None)[1]
        o_pair = jnp.where(lane < 64, acc_a[:, :128] / acc_a[:, 128:], acc_b[:, :128] / acc_b[:, 128:])
        o_ref[:, sl] = o_pair.astype(BF16)


def _na_ctx(cols):
    (qa, qt), (ka, kt), (va, vt) = cols["na_q"], cols["na_k"], cols["na_v"]
    return pl.pallas_call(
        _na_ctx_kernel,
        grid=(BATCH,),
        in_specs=[
            pl.BlockSpec((SEQ, 512), lambda b: (b, qt)),
            pl.BlockSpec((SEQ, 512), lambda b: (b, kt)),
            pl.BlockSpec((SEQ, 512), lambda b: (b, vt)),
        ],
        out_specs=pl.BlockSpec((SEQ, 512), lambda b: (b, 0)),
        out_shape=jax.ShapeDtypeStruct((BATCH * SEQ, 512), BF16),
        compiler_params=_cparams(("arbitrary",), 32),
        name="nbr_attn_ctx",
    )(qa, ka, va)


def _log_sigmoid(x):
    return jnp.minimum(x, 0.0) - jnp.log1p(jnp.exp(-jnp.abs(x)))


def _split3(x):
    x1 = x.astype(BF16)
    r1 = x - x1.astype(F32)
    x2 = r1.astype(BF16)
    x3 = (r1 - x2.astype(F32)).astype(BF16)
    return x1, x2, x3


def _ml_kernel(*refs, t_len, nb, unroll, zero_init, emit_state):
    q_ref, k_ref, v_ref, og_ref, g_ref = refs[:5]
    pos = 5
    if not zero_init:
        c0_ref, n0_ref, m0_ref = refs[pos:pos + 3]
        pos += 3
    nrm_ref = refs[pos]
    pos += 2 if emit_state else 1
    o_ref = refs[pos]
    pos += 1
    if emit_state:
        cst_ref, nst_ref, mst_ref = refs[pos:pos + 3]
        pos += 3
    hsum_scr, colb_scr, rowt_scr, s_scr, m_scr = refs[pos:pos + 5]
    step = pl.program_id(0)
    lc = ML_LC
    nc = t_len // lc
    chains = [(bi, h, d) for bi in range(nb) for h in range(ML_HEADS) for d in range(2)]

    ri = lax.broadcasted_iota(I32, (lc, lc), 0)
    ci = lax.broadcasted_iota(I32, (lc, lc), 1)
    lo = ri >= ci
    up = ri <= ci
    lo_b = jnp.where(lo, 1.0, 0.0).astype(BF16)
    up_b = jnp.where(up, 1.0, 0.0).astype(BF16)
    lane = lax.broadcasted_iota(I32, (lc, 128), 1)
    is_forget = (lane & 1) == 1
    is_bwd = (lane & 2) == 2

    def tri_left(tri, x):
        return sum(_dot(tri, p) for p in _split3(x))

    def prep(c, carry):
        r0 = pl.multiple_of(c * lc, lc)
        g = g_ref[pl.ds(r0, lc), :]
        lf = _log_sigmoid(g)
        cb = jnp.where(is_forget, jnp.where(is_bwd, tri_left(up_b, lf), tri_left(lo_b, lf)), g)
        colb_scr[pl.ds(r0, lc), :] = cb
        rowt_scr[c] = (pltpu.roll(cb, 127, 1) - cb).T[0:16]
        return carry

    lax.fori_loop(0, nb * nc, prep, 0, unroll=2)

    hsum_scr[...] = jnp.zeros_like(hsum_scr)
    if zero_init:
        s_scr[...] = jnp.zeros_like(s_scr)
        m_scr[...] = jnp.zeros_like(m_scr)
    else:
        for idx, (bi, h, d) in enumerate(chains):
            n_rep = jnp.broadcast_to(n0_ref[d, h:h + 1, :], (ML_DIM, ML_DIM)).T
            s_scr[idx * ML_DIM:(idx + 1) * ML_DIM, :] = jnp.concatenate([c0_ref[d, h], n_rep], axis=1)
            m0 = m0_ref[((step * nb + bi) * 2 + d) * ML_HEADS + h]
            m_scr[idx:idx + 1, :] = jnp.full((1, 128), m0, F32)

    ones_b = jnp.ones((lc, ML_DIM), BF16)
    n_ch = len(chains)

    def body(i, carry):
        cs = [i if d == 0 else nc - 1 - i for (_, _, d) in chains]
        r0s = [pl.multiple_of(bi * t_len + c * lc, lc) for (bi, _, _), c in zip(chains, cs)]
        hsl = [slice(h * 128, (h + 1) * 128) for (_, h, _) in chains]
        qs = [q_ref[pl.ds(r0, lc), sl] for r0, sl in zip(r0s, hsl)]
        ks = [k_ref[pl.ds(r0, lc), sl] for r0, sl in zip(r0s, hsl)]
        vs = [v_ref[pl.ds(r0, lc), sl] for r0, sl in zip(r0s, hsl)]
        s_old = [s_scr[idx * ML_DIM:(idx + 1) * ML_DIM, :] for idx in range(n_ch)]
        qk = [_dot_nt(q, k) for q, k in zip(qs, ks)]

        lhs, mts, bcs, ics, mms = [], [], [], [], []
        for idx, (bi, h, d) in enumerate(chains):
            l0 = h * 4 + d * 2
            cb = colb_scr[pl.ds(r0s[idx], lc), :]
            ic = cb[:, l0:l0 + 1]
            bc = cb[:, l0 + 1:l0 + 2]
            arow = rowt_scr[bi * nc + cs[idx], l0:l0 + 1, :]
            mm = m_scr[idx:idx + 1, 0:1]
            logd = jnp.where(lo if d == 0 else up, bc - arow, -jnp.inf)
            mt = jnp.maximum(bc + mm, logd.max(axis=1, keepdims=True))
            sc = qk[idx] * jnp.exp(logd - mt)
            inter = jnp.exp(bc + mm - mt)
            lhs.append(jnp.concatenate([(inter * qs[idx].astype(F32)).astype(BF16), sc.astype(BF16)], axis=1))
            mts.append(mt)
            bcs.append(bc)
            ics.append(ic)
            mms.append(mm)

        nds = [_dot(a, jnp.concatenate([so.astype(BF16), jnp.concatenate([v, ones_b], axis=1)], axis=0))
               for a, so, v in zip(lhs, s_old, vs)]

        hcs, wvs, decays, m_news = [], [], [], []
        for idx, (bi, h, d) in enumerate(chains):
            last = lc - 1 if d == 0 else 0
            nd = nds[idx]
            hcs.append(nd[:, :ML_DIM] / jnp.maximum(jnp.abs(nd[:, ML_DIM:]), jnp.exp(-mts[idx])))
            m_new = mts[idx][last:last + 1, :]
            bl = bcs[idx][last:last + 1, :]
            w = jnp.exp(bl - bcs[idx] + ics[idx] - m_new)
            decays.append(jnp.exp(bl + mms[idx] - m_new))
            m_news.append(jnp.broadcast_to(m_new, (1, 128)))
            wvs.append(jnp.concatenate([(w * vs[idx].astype(F32)).astype(BF16),
                                        jnp.broadcast_to(w, (lc, ML_DIM)).astype(BF16)], axis=1))
        kts = [k.astype(F32).T.astype(BF16) for k in ks]
        kv = [_dot(kt, wv) for kt, wv in zip(kts, wvs)]
        s_scr[...] = jnp.concatenate([dc * so + x for dc, so, x in zip(decays, s_old, kv)], axis=0)
        m_scr[...] = jnp.concatenate(m_news, axis=0)
        for bi in range(nb):
            for d in range(2):
                sel = [idx for idx, ch in enumerate(chains) if ch[0] == bi and ch[2] == d]
                hsum_scr[pl.ds(r0s[sel[0]], lc), :] += jnp.concatenate([hcs[idx] for idx in sel], axis=1)
        return carry

    lax.fori_loop(0, nc, body, 0, unroll=unroll)

    for bi in range(nb):
        rows = slice(bi * t_len, (bi + 1) * t_len)
        for h in range(ML_HEADS):
            sl = slice(h * 128, (h + 1) * 128)
            hs = hsum_scr[rows, sl]
            ms = jnp.mean(hs * hs, axis=-1, keepdims=True)
            y = hs * lax.rsqrt(ms + EPS) * nrm_ref[:, sl]
            o_ref[rows, sl] = (y * jax.nn.sigmoid(og_ref[rows, sl].astype(F32))).astype(BF16)
    if emit_state:
        for idx, (bi, h, d) in enumerate(chains):
            st = s_scr[idx * ML_DIM:(idx + 1) * ML_DIM, :]
            cst_ref[bi, d, h] = st[:, :ML_DIM]
            nst_ref[bi, d, h:h + 1, :] = st[:, ML_DIM:].T[0:1, :]
            mst_ref[bi, d, h:h + 1, :] = m_scr[idx:idx + 1, :]


def _mlstm(cols, gates, states, c_out, l, ml_norm_l, *, prompt):
    names = ("ml_q", "ml_k", "ml_v", "ml_o")
    m = cols["ml_q"][0].shape[0]
    t_len = SEQ if prompt else DEC_SEQ
    nb = 2 if prompt else 1
    n_seq = m // t_len
    rows = nb * t_len
    big = {} if prompt else dict(pipeline_mode=pl.Buffered(1))
    in_specs = [pl.BlockSpec((rows, 512), functools.partial(lambda i, t: (i, t), t=cols[n][1]), **big)
                for n in names]
    in_specs.append(pl.BlockSpec((rows, 128), lambda i: (i, 0)))
    args = [cols[n][0] for n in names] + [gates]
    if not prompt:
        c0, n0, m0 = states
        in_specs += [
            pl.BlockSpec((None, None, 2, ML_HEADS, ML_DIM, ML_DIM), lambda i: (i, l, 0, 0, 0, 0)),
            pl.BlockSpec((None, None, 2, ML_HEADS, ML_DIM), lambda i: (i, l, 0, 0, 0)),
            pl.BlockSpec(memory_space=pltpu.SMEM),
        ]
        args += [c0, n0, m0[:, l].reshape(-1)]
    in_specs.append(pl.BlockSpec((1, 512), lambda i: (0, 0)))
    args.append(ml_norm_l.reshape(1, 512))
    out_shape = [jax.ShapeDtypeStruct((m, 512), BF16)]
    out_specs = [pl.BlockSpec((rows, 512), lambda i: (i, 0))]
    aliases = {}
    if prompt:
        aliases = {len(args): 1}
        in_specs.append(pl.BlockSpec(memory_space=pl.ANY))
        args.append(c_out)
        out_shape += [
            jax.ShapeDtypeStruct(c_out.shape, F32),
            jax.ShapeDtypeStruct((n_seq, 2, ML_HEADS, ML_DIM), F32),
            jax.ShapeDtypeStruct((n_seq, 2, ML_HEADS, 128), F32),
        ]
        out_specs += [
            pl.BlockSpec((nb, None, 2, ML_HEADS, ML_DIM, ML_DIM), lambda i: (i, l, 0, 0, 0, 0)),
            pl.BlockSpec((nb, 2, ML_HEADS, ML_DIM), lambda i: (i, 0, 0, 0)),
            pl.BlockSpec((nb, 2, ML_HEADS, 128), lambda i: (i, 0, 0, 0)),
        ]
    n_chain = nb * ML_HEADS * 2
    return pl.pallas_call(
        functools.partial(_ml_kernel, t_len=t_len, nb=nb, unroll=2 if prompt else 1,
                          zero_init=prompt, emit_state=prompt),
        grid=(n_seq // nb,),
        in_specs=in_specs,
        out_specs=out_specs,
        out_shape=out_shape,
        input_output_aliases=aliases,
        scratch_shapes=[pltpu.VMEM((rows, 512), F32), pltpu.VMEM((rows, 128), F32),
                        pltpu.VMEM((rows // ML_LC, 16, ML_LC), F32),
                        pltpu.VMEM((n_chain * ML_DIM, 2 * ML_DIM), F32), pltpu.VMEM((n_chain, 128), F32)],
        compiler_params=_cparams(("arbitrary",), 56),
        name="mlstm_ctx" if prompt else "mlstm_lat",
    )(*args)


def _merge_kernel(x_ref, mod_ref, oda_ref, oml_ref, ona_ref, g0_ref, g1_ref, g2_ref,
                  wda_ref, wml_ref, wna_ref, wo_ref, o_ref):
    def branch(o, w, g):
        return jax.nn.sigmoid(g[...].astype(F32)) * _dot(o[...], w[...])

    merged = (branch(oda_ref, wda_ref, g0_ref) + branch(oml_ref, wml_ref, g1_ref)
              + branch(ona_ref, wna_ref, g2_ref))
    o_ref[...] = x_ref[...] + mod_ref[2:3, :] * _dot(merged.astype(BF16), wo_ref[...])


def _merge_out(x, modl, o_da, o_ml, o_na, cols, w_da, w_ml, w_na, w_out, *, prompt):
    m = x.shape[0]
    tm = 512
    const = lambda i: (0, 0)
    a, gt = cols["merge"]
    g0 = gt // 2
    return pl.pallas_call(
        _merge_kernel,
        grid=(m // tm,),
        in_specs=[
            pl.BlockSpec((tm, D_MODEL), lambda i: (i, 0)),
            pl.BlockSpec((None, 6, D_MODEL), _mod_row_map(prompt, tm)),
            pl.BlockSpec((tm, 512), lambda i: (i, 0)),
            pl.BlockSpec((tm, 512), lambda i: (i, 0)),
            pl.BlockSpec((tm, 512), lambda i: (i, 0)),
            pl.BlockSpec((tm, D_MODEL), lambda i: (i, g0)),
            pl.BlockSpec((tm, D_MODEL), lambda i: (i, g0 + 1)),
            pl.BlockSpec((tm, D_MODEL), lambda i: (i, g0 + 2)),
            pl.BlockSpec((512, D_MODEL), const),
            pl.BlockSpec((512, D_MODEL), const),
            pl.BlockSpec((512, D_MODEL), const),
            pl.BlockSpec((D_MODEL, D_MODEL), const),
        ],
        out_specs=pl.BlockSpec((tm, D_MODEL), lambda i: (i, 0)),
        out_shape=jax.ShapeDtypeStruct((m, D_MODEL), F32),
        compiler_params=_cparams(("arbitrary",), 48),
        name="merge_out_ctx" if prompt else "merge_out_lat",
    )(x, modl, o_da, o_ml, o_na, a, a, a, w_da, w_ml, w_na, w_out)


def _ffn_kernel(x_ref, mod_ref, g_ref, w1_ref, b1_ref, w2_ref, b2_ref, o_ref, h_scr, acc_scr):
    f = pl.program_id(1)

    @pl.when(f == 0)
    def _():
        h_scr[...] = _modulated_norm(x_ref[...], g_ref[...], mod_ref[3:4, :], mod_ref[4:5, :]).astype(BF16)
        acc_scr[...] = jnp.zeros_like(acc_scr)

    hr = h_scr.shape[0] // 2
    for r in range(2):
        rs = slice(r * hr, (r + 1) * hr)
        u = jnp.maximum(_dot(h_scr[rs, :], w1_ref[...]) + b1_ref[...], 0.0)
        acc_scr[rs, :] += _dot((u * u).astype(BF16), w2_ref[...])

    @pl.when(f == pl.num_programs(1) - 1)
    def _():
        o_ref[...] = x_ref[...] + mod_ref[5:6, :] * (acc_scr[...] + b2_ref[...])


def _ffn(x, modl, g2, w1, b1, w2, b2, *, prompt):
    m = x.shape[0]
    tm, tf = 1024, 512
    return pl.pallas_call(
        _ffn_kernel,
        grid=(m // tm, D_FF // tf),
        in_specs=[
            pl.BlockSpec((tm, D_MODEL), lambda i, f: (i, 0)),
            pl.BlockSpec((None, 6, D_MODEL), _mod_row_map(prompt, tm)),
            pl.BlockSpec((1, D_MODEL), lambda i, f: (0, 0)),
            pl.BlockSpec((D_MODEL, tf), lambda i, f: (0, f)),
            pl.BlockSpec((1, tf), lambda i, f: (0, f)),
            pl.BlockSpec((tf, D_MODEL), lambda i, f: (f, 0)),
            pl.BlockSpec((1, D_MODEL), lambda i, f: (0, 0)),
        ],
        out_specs=pl.BlockSpec((tm, D_MODEL), lambda i, f: (i, 0)),
        out_shape=jax.ShapeDtypeStruct((m, D_MODEL), F32),
        scratch_shapes=[pltpu.VMEM((tm, D_MODEL), BF16), pltpu.VMEM((tm, D_MODEL), F32)],
        compiler_params=_cparams(("arbitrary", "arbitrary"), 48),
        name="ffn_ctx" if prompt else "ffn_lat",
    )(x, modl, g2, w1, b1, w2, b2)


def _final_norm_kernel(x_ref, g_ref, o_ref):
    x = x_ref[...]
    ms = jnp.mean(x * x, axis=-1, keepdims=True)
    o_ref[...] = x * lax.rsqrt(ms + EPS) * g_ref[...]


def _final_norm(x, g):
    m = x.shape[0]
    tm = 1024
    return pl.pallas_call(
        _final_norm_kernel,
        grid=(m // tm,),
        in_specs=[pl.BlockSpec((tm, D_MODEL), lambda i: (i, 0)),
                  pl.BlockSpec((1, D_MODEL), lambda i: (0, 0))],
        out_specs=pl.BlockSpec((tm, D_MODEL), lambda i: (i, 0)),
        out_shape=jax.ShapeDtypeStruct((m, D_MODEL), F32),
        compiler_params=_cparams(("arbitrary",), 32),
        name="final_norm",
    )(x, g)


def _rope_tables():
    t = jnp.arange(DEC_SEQ)
    lane = jnp.arange(128)
    sub = lane % 32
    freq = ROPE_BASE ** (-(2 * (sub % 16)).astype(F32) / 32.0)
    use_row = (lane % 64) < 32
    posv = jnp.where(use_row[None, :], (t // GRID_W)[:, None], (t % GRID_W)[:, None]).astype(F32)
    ang = posv * freq[None, :]
    sign = jnp.where(sub < 16, -1.0, 1.0).astype(F32)
    return jnp.cos(ang), jnp.sin(ang) * sign[None, :]


def kernel(x_prompt, x_sample, cache_da_k, cache_da_v, cache_na_k, cache_na_v, state_ml_C, state_ml_n,
           state_ml_m, c, c_ctx, w_mod, b_mod, norm1, w_in, b_in, da_lam, da_subln, ml_norm, na_rpb,
           w_up_da, w_up_ml, w_up_na, w_out, norm2, w_ff1, b_ff1, w_ff2, b_ff2, norm_f):
    xp = x_prompt.reshape(BATCH * SEQ, D_MODEL)
    xs = x_sample.reshape(DEC_BATCH * DEC_SEQ, D_MODEL)

    segs = ((512, 1024), (1024, 1536), (4112, 4624), (4624, 5136), (0, 512), (1536, GATE_OFF),
            (GATE_OFF + 16, 4112), (5136, 8208))
    w_main = jnp.concatenate([w_in[..., a:b] for a, b in segs], axis=-1).astype(BF16)
    b_main = jnp.concatenate([b_in[..., a:b] for a, b in segs], axis=-1).reshape(DEPTH, 1, N_MAIN)
    tiles = ("da_k", "da_v", "na_k", "na_v", "da_q", "ml_q", "ml_k", "ml_v", "ml_o", "na_q", "merge")
    tile_of = {n: t for t, n in enumerate(tiles)}
    src, dst = [], []
    for hh in range(ML_HEADS):
        for dr in range(2):
            for tt in range(2):
                src.append(GATE_OFF + dr * 2 * ML_HEADS + tt * ML_HEADS + hh)
                dst.append(hh * 4 + dr * 2 + tt)
    src, dst = np.array(src), np.array(dst)
    w_g = jnp.zeros((DEPTH, D_MODEL, 128), F32).at[:, :, dst].set(w_in[:, :, src]).astype(BF16)
    b_g = jnp.zeros((DEPTH, 1, 128), F32).at[:, 0, dst].set(b_in[:, src])
    cs = np.ones((1, N_MAIN), np.float32)
    for name, scale in (("da_q", DA_QK ** -0.5), ("ml_k", ML_DIM ** -0.5), ("na_q", NA_DIM ** -0.5)):
        cs[0, tile_of[name] * 512:(tile_of[name] + 1) * 512] = scale
    cscale = jnp.asarray(cs)
    w_da_b, w_ml_b, w_na_b, w_out_b = (w.astype(BF16) for w in (w_up_da, w_up_ml, w_up_na, w_out))
    w_ff1_b, w_ff2_b = w_ff1.astype(BF16), w_ff2.astype(BF16)
    rope_tabs = _rope_tables()

    cc = jnp.zeros((8, D_MODEL), F32).at[0].set(c_ctx).at[1:1 + DEC_BATCH].set(c)
    mod = _modulation(cc, w_mod, b_mod).reshape(DEPTH, 8, 6, D_MODEL)

    cdk = cache_da_k.reshape(DEC_BATCH, DEPTH, PAST_LEN, 512)
    cdv = cache_da_v.reshape(DEC_BATCH, DEPTH, PAST_LEN, 512)
    cnk = cache_na_k.reshape(DEC_BATCH, DEPTH, PAST_LEN, 512)
    cnv = cache_na_v.reshape(DEC_BATCH, DEPTH, PAST_LEN, 512)

    kv_outs = [jnp.zeros((BATCH, DEPTH, SEQ, 512), F32) for _ in range(4)]
    c_out = jnp.zeros((BATCH, DEPTH, 2, ML_HEADS, ML_DIM, ML_DIM), F32)
    coll_n, coll_m = [], []
    for l in range(DEPTH):
        lam_init = 0.8 - 0.6 * math.exp(-0.3 * l)
        modl = mod[l]
        g1 = norm1[l].reshape(1, D_MODEL)
        g2 = norm2[l].reshape(1, D_MODEL)
        sub_row = da_subln[l].reshape(1, 128)
        bias_tab = _na_bias_table(na_rpb[l])
        for prompt in (True, False):
            x = xp if prompt else xs
            if prompt:
                a_kv, kv_outs = _in_proj_kv(x, modl, g1, w_main[l], b_main[l], kv_outs, l)
                a, gates = _in_proj(x, modl, g1, w_main[l], b_main[l], cscale, w_g[l], b_g[l],
                                    col0=N_KV, prompt=True)
                cols = {n: (a_kv, t) if t < 4 else (a, t - 4) for n, t in tile_of.items()}
            else:
                a, gates = _in_proj(x, modl, g1, w_main[l], b_main[l], cscale, w_g[l], b_g[l],
                                    col0=0, prompt=False)
                cols = {n: (a, t) for n, t in tile_of.items()}
            o_da = _diff_attention(cols, cdk, cdv, rope_tabs, l, da_lam[l], sub_row, lam_init, prompt=prompt)
            ml_out = _mlstm(cols, gates, (state_ml_C, state_ml_n, state_ml_m), c_out, l, ml_norm[l],
                            prompt=prompt)
            o_ml = ml_out[0]
            if prompt:
                c_out = ml_out[1]
                coll_n.append(ml_out[2])
                coll_m.append(ml_out[3][..., 0])
                o_na = _na_ctx(cols)
            else:
                o_na = _na_lat(cols, cnk, cnv, l, bias_tab)
            x = _merge_out(x, modl, o_da, o_ml, o_na, cols, w_da_b[l], w_ml_b[l], w_na_b[l], w_out_b[l],
                           prompt=prompt)
            x = _ffn(x, modl, g2, w_ff1_b[l], b_ff1[l].reshape(1, D_FF), w_ff2_b[l],
                     b_ff2[l].reshape(1, D_MODEL), prompt=prompt)
            if prompt:
                xp = x
            else:
                xs = x

    gf = norm_f.reshape(1, D_MODEL)
    y_prompt = _final_norm(xp, gf).reshape(BATCH, SEQ, D_MODEL)
    y_sample = _final_norm(xs, gf).reshape(DEC_BATCH, DEC_SEQ, D_MODEL)
    new_da_k = kv_outs[0].reshape(BATCH, DEPTH, SEQ, DA_HEADS, 2 * DA_QK)
    new_da_v = kv_outs[1].reshape(BATCH, DEPTH, SEQ, DA_HEADS, 2 * DA_QK)
    new_na_k = kv_outs[2].reshape(BATCH, DEPTH, SEQ, NA_HEADS, NA_DIM)
    new_na_v = kv_outs[3].reshape(BATCH, DEPTH, SEQ, NA_HEADS, NA_DIM)
    new_ml_n = jnp.stack(coll_n, axis=1)
    new_ml_m = jnp.stack(coll_m, axis=1)
    return (y_prompt, y_sample, new_da_k, new_da_v, new_na_k, new_na_v, c_out, new_ml_n, new_ml_m)
```

```python
import functools
import math

import jax
import jax.numpy as jnp
import numpy as np
from jax import lax
from jax.experimental import pallas as pl
from jax.experimental.pallas import tpu as pltpu

F32 = jnp.float32
BF16 = jnp.bfloat16
I32 = jnp.int32

D_MODEL = 1024
BATCH = 32
SEQ = 256
DEPTH = 4
DEC_BATCH = 2
DEC_SEQ = 4096
PAST_LEN = 512
GRID_W = 64
DA_HEADS = 4
DA_QK = 64
ML_HEADS = 4
ML_DIM = 128
ML_LC = 128
NA_HEADS = 8
NA_DIM = 64
NA_WIN_ROWS = 8
NA_WIN_COLS = 16
D_FF = 4 * D_MODEL
ROPE_BASE = 10000.0
EPS = 1e-6
N_MAIN = 8192
GATE_OFF = 3584
NEG = -1e30

MIB = 1024 * 1024
NT_DIMS = (((1,), (1,)), ((), ()))


def _cparams(sem, vmem_mib):
    return pltpu.CompilerParams(dimension_semantics=sem, vmem_limit_bytes=vmem_mib * MIB)


def _dot(a, b):
    return jnp.dot(a, b, preferred_element_type=F32)


def _dot_nt(a, b):
    return lax.dot_general(a, b, NT_DIMS, preferred_element_type=F32)


def _mod_kernel(c_ref, w_ref, b_ref, o_ref):
    c = c_ref[...]
    s = (c * jax.nn.sigmoid(c)).astype(BF16)
    o_ref[0] = _dot(s, w_ref[0].astype(BF16)) + b_ref[0]


def _modulation(cc, w_mod, b_mod):
    tn = 1536
    n = 6 * D_MODEL
    return pl.pallas_call(
        _mod_kernel,
        grid=(DEPTH, n // tn),
        in_specs=[
            pl.BlockSpec((8, D_MODEL), lambda l, j: (0, 0)),
            pl.BlockSpec((1, D_MODEL, tn), lambda l, j: (l, 0, j)),
            pl.BlockSpec((1, 1, tn), lambda l, j: (l, 0, j)),
        ],
        out_specs=pl.BlockSpec((1, 8, tn), lambda l, j: (l, 0, j)),
        out_shape=jax.ShapeDtypeStruct((DEPTH, 8, n), F32),
        compiler_params=_cparams(("arbitrary", "arbitrary"), 40),
        name="modulation",
    )(cc, w_mod, b_mod.reshape(DEPTH, 1, n))


def _modulated_norm(x, g, shift, scale):
    ms = jnp.mean(x * x, axis=-1, keepdims=True)
    y = x * lax.rsqrt(ms + EPS) * g
    return y * (1.0 + scale) + shift


IP_TM = 1024
IP_SUB = 512
N_KV = 2048


def _inproj_kernel(x_ref, mod_ref, g_ref, w_ref, b_ref, cs_ref, wg_ref, bg_ref, a_ref, gate_ref, h_scr):
    @pl.when(pl.program_id(1) == 0)
    def _():
        h = _modulated_norm(x_ref[...], g_ref[...], mod_ref[0:1, :], mod_ref[1:2, :]).astype(BF16)
        h_scr[...] = h
        gate_ref[...] = _dot(h, wg_ref[...]) + bg_ref[...]

    h = h_scr[...]
    for c in range(a_ref.shape[1] // IP_SUB):
        cs = slice(c * IP_SUB, (c + 1) * IP_SUB)
        a_ref[:, cs] = ((_dot(h, w_ref[:, cs]) + b_ref[:, cs]) * cs_ref[:, cs]).astype(BF16)


def _inproj_kv_kernel(x_ref, mod_ref, g_ref, w_ref, b_ref, *refs):
    a_ref = refs[4]
    f32_refs = refs[5:9]
    h = _modulated_norm(x_ref[...], g_ref[...], mod_ref[0:1, :], mod_ref[1:2, :]).astype(BF16)
    for c, ref in enumerate(f32_refs):
        cs = slice(c * IP_SUB, (c + 1) * IP_SUB)
        acc = _dot(h, w_ref[:, cs]) + b_ref[:, cs]
        ref[...] = acc.reshape(ref.shape)
        a_ref[:, cs] = acc.astype(BF16)


def _mod_row_map(prompt, tm):
    if prompt:
        return lambda i, *_: (0, 0, 0)
    return lambda i, *_: (1 + i // (DEC_SEQ // tm), 0, 0)


def _in_proj(x, modl, g1, w_main, b_main, cscale, w_g, b_g, *, col0, prompt):
    m = x.shape[0]
    tm, tn = IP_TM, 1024
    j0 = col0 // tn
    return pl.pallas_call(
        _inproj_kernel,
        grid=(m // tm, (N_MAIN - col0) // tn),
        in_specs=[
            pl.BlockSpec((tm, D_MODEL), lambda i, j: (i, 0)),
            pl.BlockSpec((None, 6, D_MODEL), _mod_row_map(prompt, tm)),
            pl.BlockSpec((1, D_MODEL), lambda i, j: (0, 0)),
            pl.BlockSpec((D_MODEL, tn), lambda i, j: (0, j + j0)),
            pl.BlockSpec((1, tn), lambda i, j: (0, j + j0)),
            pl.BlockSpec((1, tn), lambda i, j: (0, j + j0)),
            pl.BlockSpec((D_MODEL, 128), lambda i, j: (0, 0)),
            pl.BlockSpec((1, 128), lambda i, j: (0, 0)),
        ],
        out_specs=[pl.BlockSpec((tm, tn), lambda i, j: (i, j)),
                   pl.BlockSpec((tm, 128), lambda i, j: (i, 0))],
        out_shape=[jax.ShapeDtypeStruct((m, N_MAIN - col0), BF16), jax.ShapeDtypeStruct((m, 128), F32)],
        scratch_shapes=[pltpu.VMEM((tm, D_MODEL), BF16)],
        compiler_params=_cparams(("arbitrary", "arbitrary"), 48),
        name="in_proj_ctx" if prompt else "in_proj_lat",
    )(x, modl, g1, w_main, b_main, cscale, w_g, b_g)


def _in_proj_kv(x, modl, g1, w_main, b_main, kv_outs, l):
    m = x.shape[0]
    tm = IP_TM
    nbat = tm // SEQ
    kv_spec = pl.BlockSpec((nbat, None, SEQ, IP_SUB), lambda i: (i, l, 0, 0))
    outs = pl.pallas_call(
        _inproj_kv_kernel,
        grid=(m // tm,),
        in_specs=[
            pl.BlockSpec((tm, D_MODEL), lambda i: (i, 0)),
            pl.BlockSpec((None, 6, D_MODEL), _mod_row_map(True, tm)),
            pl.BlockSpec((1, D_MODEL), lambda i: (0, 0)),
            pl.BlockSpec((D_MODEL, N_KV), lambda i: (0, 0)),
            pl.BlockSpec((1, N_KV), lambda i: (0, 0)),
        ] + [pl.BlockSpec(memory_space=pl.ANY)] * 4,
        out_specs=[pl.BlockSpec((tm, N_KV), lambda i: (i, 0))] + [kv_spec] * 4,
        out_shape=[jax.ShapeDtypeStruct((m, N_KV), BF16)]
                  + [jax.ShapeDtypeStruct(o.shape, o.dtype) for o in kv_outs],
        input_output_aliases={5 + c: 1 + c for c in range(4)},
        compiler_params=_cparams(("arbitrary",), 56),
        name="in_proj_kv_ctx",
    )(x, modl, g1, w_main, b_main, *kv_outs)
    return outs[0], outs[1:]


def _half_masked(q, upper):
    lane = lax.broadcasted_iota(I32, q.shape, 1)
    keep = (lane >= 64) if upper else (lane < 64)
    return jnp.where(keep, q, jnp.zeros_like(q))


def _da_block(qm, k_t, va, m_old, acc_old):
    return _sm_block(_dot(qm, k_t), va, m_old, acc_old)


def _sm_block(s, va, m_old, acc_old):
    nl = s.shape[1] // 128
    mx = s[:, 0:128]
    for c in range(1, nl):
        mx = jnp.maximum(mx, s[:, c * 128:(c + 1) * 128])
    m_new = jnp.broadcast_to(mx.max(axis=1, keepdims=True), mx.shape)
    if m_old is not None:
        m_new = jnp.maximum(m_old, m_new)
    e = jnp.concatenate([jnp.exp(s[:, c * 128:(c + 1) * 128] - m_new) for c in range(nl)], axis=1)
    pv = _dot(e.astype(BF16), va)
    if m_old is None:
        return m_new, pv
    alpha = jnp.exp(m_old - m_new)
    return m_new, acc_old * jnp.concatenate([alpha, alpha], axis=1) + pv


def _rope(x, cos, sin):
    lane = lax.broadcasted_iota(I32, x.shape, 1)
    partner = jnp.where((lane & 16) == 0, pltpu.roll(x, 112, 1), pltpu.roll(x, 16, 1))
    return x * cos + partner * sin


def _da_kernel(*refs, hp, s_new, s_cache, kb, rope, lam_init):
    q_ref, k_ref, v_ref = refs[:3]
    pos = 3
    if s_cache:
        ck_ref, cv_ref = refs[pos:pos + 2]
        pos += 2
    if rope:
        cos_ref, sin_ref = refs[pos:pos + 2]
        pos += 2
    lam_ref, sub_ref, o_ref, kt_scr, va_scr, m_scr, acc_scr = refs[pos:pos + 7]
    qi = pl.program_id(2)
    tq = q_ref.shape[0]
    n_new, n_all = s_new // kb, (s_new + s_cache) // kb

    @pl.when(qi == 0)
    def _():
        for hh in range(hp):
            sl = slice(hh * 128, (hh + 1) * 128)
            for j in range(n_new):
                rows = slice(j * kb, (j + 1) * kb)
                kblk = k_ref[rows, sl].astype(F32)
                if rope:
                    kblk = _rope(kblk, cos_ref[rows, :], sin_ref[rows, :])
                kt_scr[hh, j] = kblk.T.astype(BF16)
            va_scr[hh, 0:s_new, 0:128] = v_ref[:, sl]
            for j in range(n_all - n_new):
                kt_scr[hh, n_new + j] = ck_ref[j * kb:(j + 1) * kb, sl].T.astype(BF16)
            if s_cache:
                va_scr[hh, s_new:s_new + s_cache, 0:128] = cv_ref[:, sl].astype(BF16)
            va_scr[hh, :, 128:256] = jnp.ones((s_new + s_cache, 128), BF16)

    lv = lam_ref[...]
    lam = (jnp.exp(jnp.sum(lv[0:1] * lv[1:2], axis=1, keepdims=True))
           - jnp.exp(jnp.sum(lv[2:3] * lv[3:4], axis=1, keepdims=True)) + lam_init)
    for hh in range(hp):
        sl = slice(hh * 128, (hh + 1) * 128)
        q = q_ref[:, sl]
        if rope:
            q_rows = pl.ds(pl.multiple_of(qi * tq, tq), tq)
            q = _rope(q.astype(F32), cos_ref[q_rows, :], sin_ref[q_rows, :]).astype(BF16)
        qms = (_half_masked(q, False), _half_masked(q, True))
        if n_all == 1:
            accs = [_da_block(qm, kt_scr[hh, 0], va_scr[hh], None, None)[1] for qm in qms]
        else:
            m_scr[...] = jnp.full(m_scr.shape, NEG, F32)
            acc_scr[...] = jnp.zeros_like(acc_scr)

            def body(j, carry, hh=hh, qms=qms):
                va = va_scr[hh, pl.ds(pl.multiple_of(j * kb, kb), kb), :]
                k_t = kt_scr[hh, j]
                new = [_da_block(qm, k_t, va, m_scr[mp], acc_scr[mp]) for mp, qm in enumerate(qms)]
                for mp, (m_new, acc) in enumerate(new):
                    m_scr[mp] = m_new
                    acc_scr[mp] = acc
                return carry

            lax.fori_loop(0, n_all, body, 0, unroll=True)
            accs = [acc_scr[0], acc_scr[1]]
        o1 = accs[0][:, :128] / accs[0][:, 128:]
        o2 = accs[1][:, :128] / accs[1][:, 128:]
        o = o1 - lam * o2
        ms = jnp.mean(o * o, axis=1, keepdims=True)
        on = o * lax.rsqrt(ms + EPS) * sub_ref[...] * (1.0 - lam_init)
        o_ref[:, sl] = on.astype(BF16)


def _diff_attention(cols, cache_k, cache_v, rope_tabs, l, da_lam_l, subln_row, lam_init, *, prompt):
    (qa, qt), (ka, kt), (va, vt) = cols["da_q"], cols["da_k"], cols["da_v"]
    m = qa.shape[0]
    if prompt:
        hp, tq, s_new, s_cache, kb = DA_HEADS, SEQ, SEQ, 0, SEQ
        grid = (BATCH, 1, 1)
        in_specs = [
            pl.BlockSpec((SEQ, 512), lambda b, h, qi: (b, qt)),
            pl.BlockSpec((SEQ, 512), lambda b, h, qi: (b, kt)),
            pl.BlockSpec((SEQ, 512), lambda b, h, qi: (b, vt)),
        ]
        args = [qa, ka, va]
        out_spec = pl.BlockSpec((SEQ, 512), lambda b, h, qi: (b, 0))
    else:
        hp, tq, s_new, s_cache, kb = 1, 1024, DEC_SEQ, PAST_LEN, 512
        nq = DEC_SEQ // tq
        grid = (DEC_BATCH, DA_HEADS, nq)
        in_specs = [
            pl.BlockSpec((tq, 128), lambda b, h, qi: (b * nq + qi, 4 * qt + h)),
            pl.BlockSpec((DEC_SEQ, 128), lambda b, h, qi: (b, 4 * kt + h)),
            pl.BlockSpec((DEC_SEQ, 128), lambda b, h, qi: (b, 4 * vt + h)),
            pl.BlockSpec((None, None, PAST_LEN, 128), lambda b, h, qi: (b, l, 0, h)),
            pl.BlockSpec((None, None, PAST_LEN, 128), lambda b, h, qi: (b, l, 0, h)),
            pl.BlockSpec((DEC_SEQ, 128), lambda b, h, qi: (0, 0)),
            pl.BlockSpec((DEC_SEQ, 128), lambda b, h, qi: (0, 0)),
        ]
        args = [qa, ka, va, cache_k, cache_v, *rope_tabs]
        out_spec = pl.BlockSpec((tq, 128), lambda b, h, qi: (b * nq + qi, h))
    in_specs += [
        pl.BlockSpec((4, DA_QK), lambda b, h, qi: (0, 0)),
        pl.BlockSpec((1, 128), lambda b, h, qi: (0, 0)),
    ]
    args += [da_lam_l, subln_row]
    s_all = s_new + s_cache
    return pl.pallas_call(
        functools.partial(_da_kernel, hp=hp, s_new=s_new, s_cache=s_cache, kb=kb, rope=not prompt,
                          lam_init=lam_init),
        grid=grid,
        in_specs=in_specs,
        out_specs=out_spec,
        out_shape=jax.ShapeDtypeStruct((m, 512), BF16),
        scratch_shapes=[pltpu.VMEM((hp, s_all // kb, 128, kb), BF16), pltpu.VMEM((hp, s_all, 256), BF16),
                        pltpu.VMEM((2, tq, 128), F32), pltpu.VMEM((2, tq, 256), F32)],
        compiler_params=_cparams(("arbitrary", "arbitrary", "arbitrary"), 48),
        name="diff_attn_ctx" if prompt else "diff_attn_lat",
    )(*args)


def _na_bias_kernel(rpb_ref, o_ref):
    h = pl.program_id(0)
    n_dc = 2 * NA_WIN_COLS - 1
    n_dr = 2 * NA_WIN_ROWS - 1
    qc = lax.broadcasted_iota(I32, (GRID_W, 128), 0)
    lane = lax.broadcasted_iota(I32, (GRID_W, 128), 1)
    kc = lane & (GRID_W - 1)
    dcm = kc - qc + (NA_WIN_COLS - 1)
    cstart = jnp.clip(qc - NA_WIN_COLS // 2, 0, GRID_W - NA_WIN_COLS)
    left = lane < GRID_W
    base = h * (n_dr * n_dc)
    for e in range(16):
        acc = jnp.zeros((GRID_W, 128), F32)
        for dc in range(n_dc):
            lv = rpb_ref[base + (e - 1) * n_dc + dc] if e >= 1 else 0.0
            rv = rpb_ref[base + e * n_dc + dc] if e < n_dr else 0.0
            acc = jnp.where(dcm == dc, jnp.where(left, lv, rv), acc)
        o_ref[0, e] = jnp.where(kc >= cstart, jnp.where(kc < cstart + NA_WIN_COLS, acc, NEG), NEG)


def _na_bias_table(rpb_l):
    return pl.pallas_call(
        _na_bias_kernel,
        grid=(NA_HEADS,),
        in_specs=[pl.BlockSpec(memory_space=pltpu.SMEM)],
        out_specs=pl.BlockSpec((1, 16, GRID_W, 128), lambda h: (h, 0, 0, 0)),
        out_shape=jax.ShapeDtypeStruct((NA_HEADS, 16, GRID_W, 128), F32),
        compiler_params=_cparams(("arbitrary",), 16),
        name="na_bias_table",
    )(rpb_l.reshape(-1))


def _na_lat_kernel(q_ref, k_ref, v_ref, ck_ref, cv_ref, bp_ref, o_ref):
    rb = pl.program_id(1)
    ws = jnp.clip(2 * rb - 1, 0, 12)
    delta = 4 * ws - 8 * rb
    tok0 = pl.multiple_of(ws * 256, 256)
    qr = 8 * rb + (lax.broadcasted_iota(I32, (512, 1024), 0) >> 6)
    kr = 4 * ws + (lax.broadcasted_iota(I32, (512, 1024), 1) >> 6)
    st = jnp.clip(qr - NA_WIN_ROWS // 2, 0, GRID_W - NA_WIN_ROWS)
    rowmask = jnp.where(kr >= st, jnp.where(kr < st + NA_WIN_ROWS, 0.0, NEG), NEG)
    ones_b = jnp.ones((1024 + PAST_LEN, 128), BF16)
    lane = lax.broadcasted_iota(I32, (512, 128), 1)
    for g in range(NA_HEADS // 2):
        sl = slice(g * 128, (g + 1) * 128)
        q2 = q_ref[:, sl]
        kall = jnp.concatenate([k_ref[pl.ds(tok0, 1024), sl].astype(F32), ck_ref[:, sl]], axis=0)
        k_t = kall.T.astype(BF16)
        vall = jnp.concatenate([v_ref[pl.ds(tok0, 1024), sl], cv_ref[:, sl].astype(BF16)], axis=0)
        va = jnp.concatenate([vall, ones_b], axis=1)
        outs = []
        for par in range(2):
            h = 2 * g + par
            s = _dot(_half_masked(q2, par == 1), k_t)
            rows = []
            for qrl in range(8):
                tiles = [bp_ref[h, jnp.clip(delta + 2 * jk - qrl + 8, 0, 15)] for jk in range(8)]
                rows.append(jnp.concatenate(tiles, axis=1))
            bias = jnp.concatenate(rows, axis=0)
            s = jnp.concatenate([s[:, :1024] + bias + rowmask, s[:, 1024:]], axis=1)
            acc = _sm_block(s, va, None, None)[1]
            outs.append(acc[:, :128] / acc[:, 128:])
        o_ref[:, sl] = jnp.where(lane < 64, outs[0], outs[1]).astype(BF16)


def _na_lat(cols, cache_k, cache_v, l, bias_tab):
    (qa, qt), (ka, kt), (va, vt) = cols["na_q"], cols["na_k"], cols["na_v"]
    return pl.pallas_call(
        _na_lat_kernel,
        grid=(DEC_BATCH, 8),
        in_specs=[
            pl.BlockSpec((512, 512), lambda b, rb: (b * 8 + rb, qt)),
            pl.BlockSpec((DEC_SEQ, 512), lambda b, rb: (b, kt)),
            pl.BlockSpec((DEC_SEQ, 512), lambda b, rb: (b, vt)),
            pl.BlockSpec((None, None, PAST_LEN, 512), lambda b, rb: (b, l, 0, 0)),
            pl.BlockSpec((None, None, PAST_LEN, 512), lambda b, rb: (b, l, 0, 0)),
            pl.BlockSpec((NA_HEADS, 16, GRID_W, 128), lambda b, rb: (0, 0, 0, 0)),
        ],
        out_specs=pl.BlockSpec((512, 512), lambda b, rb: (b * 8 + rb, 0)),
        out_shape=jax.ShapeDtypeStruct((DEC_BATCH * DEC_SEQ, 512), BF16),
        compiler_params=_cparams(("arbitrary", "arbitrary"), 56),
        name="nbr_attn_lat",
    )(qa, ka, va, cache_k, cache_v, bias_tab)


def _na_ctx_kernel(q_ref, k_ref, v_ref, o_ref):
    ones_b = jnp.ones((q_ref.shape[0], 128), BF16)
    lane = lax.broadcasted_iota(I32, (q_ref.shape[0], 128), 1)
    for g in range(NA_HEADS // 2):
        sl = slice(g * 128, (g + 1) * 128)
        q2 = q_ref[:, sl]
        k_t = k_ref[:, sl].astype(F32).T.astype(BF16)
        va = jnp.concatenate([v_ref[:, sl], ones_b], axis=1)
        acc_a = _da_block(_half_masked(q2, False), k_t, va, None, None)[1]
        acc_b = _da_block(_half_masked(q2, True), k_t, va, None, None)[1]
        o_pair = jnp.where(lane < 64, acc_a[:, :128] / acc_a[:, 128:], acc_b[:, :128] / acc_b[:, 128:])
        o_ref[:, sl] = o_pair.astype(BF16)


def _na_ctx(cols):
    (qa, qt), (ka, kt), (va, vt) = cols["na_q"], cols["na_k"], cols["na_v"]
    return pl.pallas_call(
        _na_ctx_kernel,
        grid=(BATCH,),
        in_specs=[
            pl.BlockSpec((SEQ, 512), lambda b: (b, qt)),
            pl.BlockSpec((SEQ, 512), lambda b: (b, kt)),
            pl.BlockSpec((SEQ, 512), lambda b: (b, vt)),
        ],
        out_specs=pl.BlockSpec((SEQ, 512), lambda b: (b, 0)),
        out_shape=jax.ShapeDtypeStruct((BATCH * SEQ, 512), BF16),
        compiler_params=_cparams(("arbitrary",), 32),
        name="nbr_attn_ctx",
    )(qa, ka, va)


def _log_sigmoid(x):
    return jnp.minimum(x, 0.0) - jnp.log1p(jnp.exp(-jnp.abs(x)))


def _split3(x):
    x1 = x.astype(BF16)
    r1 = x - x1.astype(F32)
    x2 = r1.astype(BF16)
    x3 = (r1 - x2.astype(F32)).astype(BF16)
    return x1, x2, x3


def _ml_kernel(*refs, t_len, nb, unroll, zero_init, emit_state):
    q_ref, k_ref, v_ref, og_ref, g_ref = refs[:5]
    pos = 5
    if not zero_init:
        c0_ref, n0_ref, m0_ref = refs[pos:pos + 3]
        pos += 3
    nrm_ref = refs[pos]
    pos += 2 if emit_state else 1
    o_ref = refs[pos]
    pos += 1
    if emit_state:
        cst_ref, nst_ref, mst_ref = refs[pos:pos + 3]
        pos += 3
    hsum_scr, colb_scr, rowt_scr, s_scr, m_scr = refs[pos:pos + 5]
    step = pl.program_id(0)
    lc = ML_LC
    nc = t_len // lc
    chains = [(bi, h, d) for bi in range(nb) for h in range(ML_HEADS) for d in range(2)]

    ri = lax.broadcasted_iota(I32, (lc, lc), 0)
    ci = lax.broadcasted_iota(I32, (lc, lc), 1)
    lo = ri >= ci
    up = ri <= ci
    lo_b = jnp.where(lo, 1.0, 0.0).astype(BF16)
    up_b = jnp.where(up, 1.0, 0.0).astype(BF16)
    lane = lax.broadcasted_iota(I32, (lc, 128), 1)
    is_forget = (lane & 1) == 1
    is_bwd = (lane & 2) == 2

    def tri_left(tri, x):
        return sum(_dot(tri, p) for p in _split3(x))

    def prep(c, carry):
        r0 = pl.multiple_of(c * lc, lc)
        g = g_ref[pl.ds(r0, lc), :]
        lf = _log_sigmoid(g)
        cb = jnp.where(is_forget, jnp.where(is_bwd, tri_left(up_b, lf), tri_left(lo_b, lf)), g)
        colb_scr[pl.ds(r0, lc), :] = cb
        rowt_scr[c] = (pltpu.roll(cb, 127, 1) - cb).T[0:16]
        return carry

    lax.fori_loop(0, nb * nc, prep, 0, unroll=2)

    hsum_scr[...] = jnp.zeros_like(hsum_scr)
    if zero_init:
        s_scr[...] = jnp.zeros_like(s_scr)
        m_scr[...] = jnp.zeros_like(m_scr)
    else:
        for idx, (bi, h, d) in enumerate(chains):
            n_rep = jnp.broadcast_to(n0_ref[d, h:h + 1, :], (ML_DIM, ML_DIM)).T
            s_scr[idx * ML_DIM:(idx + 1) * ML_DIM, :] = jnp.concatenate([c0_ref[d, h], n_rep], axis=1)
            m0 = m0_ref[((step * nb + bi) * 2 + d) * ML_HEADS + h]
            m_scr[idx:idx + 1, :] = jnp.full((1, 128), m0, F32)

    ones_b = jnp.ones((lc, ML_DIM), BF16)
    n_ch = len(chains)

    def body(i, carry):
        cs = [i if d == 0 else nc - 1 - i for (_, _, d) in chains]
        r0s = [pl.multiple_of(bi * t_len + c * lc, lc) for (bi, _, _), c in zip(chains, cs)]
        hsl = [slice(h * 128, (h + 1) * 128) for (_, h, _) in chains]
        qs = [q_ref[pl.ds(r0, lc), sl] for r0, sl in zip(r0s, hsl)]
        ks = [k_ref[pl.ds(r0, lc), sl] for r0, sl in zip(r0s, hsl)]
        vs = [v_ref[pl.ds(r0, lc), sl] for r0, sl in zip(r0s, hsl)]
        s_old = [s_scr[idx * ML_DIM:(idx + 1) * ML_DIM, :] for idx in range(n_ch)]
        qk = [_dot_nt(q, k) for q, k in zip(qs, ks)]

        lhs, mts, bcs, ics, mms = [], [], [], [], []
        for idx, (bi, h, d) in enumerate(chains):
            l0 = h * 4 + d * 2
            cb = colb_scr[pl.ds(r0s[idx], lc), :]
            ic = cb[:, l0:l0 + 1]
            bc = cb[:, l0 + 1:l0 + 2]
            arow = rowt_scr[bi * nc + cs[idx], l0:l0 + 1, :]
            mm = m_scr[idx:idx + 1, 0:1]
            logd = jnp.where(lo if d == 0 else up, bc - arow, -jnp.inf)
            mt = jnp.maximum(bc + mm, logd.max(axis=1, keepdims=True))
            sc = qk[idx] * jnp.exp(logd - mt)
            inter = jnp.exp(bc + mm - mt)
            lhs.append(jnp.concatenate([(inter * qs[idx].astype(F32)).astype(BF16), sc.astype(BF16)], axis=1))
            mts.append(mt)
            bcs.append(bc)
            ics.append(ic)
            mms.append(mm)

        nds = [_dot(a, jnp.concatenate([so.astype(BF16), jnp.concatenate([v, ones_b], axis=1)], axis=0))
               for a, so, v in zip(lhs, s_old, vs)]

        hcs, wvs, decays, m_news = [], [], [], []
        for idx, (bi, h, d) in enumerate(chains):
            last = lc - 1 if d == 0 else 0
            nd = nds[idx]
            hcs.append(nd[:, :ML_DIM] / jnp.maximum(jnp.abs(nd[:, ML_DIM:]), jnp.exp(-mts[idx])))
            m_new = mts[idx][last:last + 1, :]
            bl = bcs[idx][last:last + 1, :]
            w = jnp.exp(bl - bcs[idx] + ics[idx] - m_new)
            decays.append(jnp.exp(bl + mms[idx] - m_new))
            m_news.append(jnp.broadcast_to(m_new, (1, 128)))
            wvs.append(jnp.concatenate([(w * vs[idx].astype(F32)).astype(BF16),
                                        jnp.broadcast_to(w, (lc, ML_DIM)).astype(BF16)], axis=1))
        kts = [k.astype(F32).T.astype(BF16) for k in ks]
        kv = [_dot(kt, wv) for kt, wv in zip(kts, wvs)]
        s_scr[...] = jnp.concatenate([dc * so + x for dc, so, x in zip(decays, s_old, kv)], axis=0)
        m_scr[...] = jnp.concatenate(m_news, axis=0)
        for bi in range(nb):
            for d in range(2):
                sel = [idx for idx, ch in enumerate(chains) if ch[0] == bi and ch[2] == d]
                hsum_scr[pl.ds(r0s[sel[0]], lc), :] += jnp.concatenate([hcs[idx] for idx in sel], axis=1)
        return carry

    lax.fori_loop(0, nc, body, 0, unroll=unroll)

    for bi in range(nb):
        rows = slice(bi * t_len, (bi + 1) * t_len)
        for h in range(ML_HEADS):
            sl = slice(h * 128, (h + 1) * 128)
            hs = hsum_scr[rows, sl]
            ms = jnp.mean(hs * hs, axis=-1, keepdims=True)
            y = hs * lax.rsqrt(ms + EPS) * nrm_ref[:, sl]
            o_ref[rows, sl] = (y * jax.nn.sigmoid(og_ref[rows, sl].astype(F32))).astype(BF16)
    if emit_state:
        for idx, (bi, h, d) in enumerate(chains):
            st = s_scr[idx * ML_DIM:(idx + 1) * ML_DIM, :]
            cst_ref[bi, d, h] = st[:, :ML_DIM]
            nst_ref[bi, d, h:h + 1, :] = st[:, ML_DIM:].T[0:1, :]
            mst_ref[bi, d, h:h + 1, :] = m_scr[idx:idx + 1, :]


def _mlstm(cols, gates, states, c_out, l, ml_norm_l, *, prompt):
    names = ("ml_q", "ml_k", "ml_v", "ml_o")
    m = cols["ml_q"][0].shape[0]
    t_len = SEQ if prompt else DEC_SEQ
    nb = 2 if prompt else 1
    n_seq = m // t_len
    rows = nb * t_len
    big = {} if prompt else dict(pipeline_mode=pl.Buffered(1))
    in_specs = [pl.BlockSpec((rows, 512), functools.partial(lambda i, t: (i, t), t=cols[n][1]), **big)
                for n in names]
    in_specs.append(pl.BlockSpec((rows, 128), lambda i: (i, 0)))
    args = [cols[n][0] for n in names] + [gates]
    if not prompt:
        c0, n0, m0 = states
        in_specs += [
            pl.BlockSpec((None, None, 2, ML_HEADS, ML_DIM, ML_DIM), lambda i: (i, l, 0, 0, 0, 0)),
            pl.BlockSpec((None, None, 2, ML_HEADS, ML_DIM), lambda i: (i, l, 0, 0, 0)),
            pl.BlockSpec(memory_space=pltpu.SMEM),
        ]
        args += [c0, n0, m0[:, l].reshape(-1)]
    in_specs.append(pl.BlockSpec((1, 512), lambda i: (0, 0)))
    args.append(ml_norm_l.reshape(1, 512))
    out_shape = [jax.ShapeDtypeStruct((m, 512), BF16)]
    out_specs = [pl.BlockSpec((rows, 512), lambda i: (i, 0))]
    aliases = {}
    if prompt:
        aliases = {len(args): 1}
        in_specs.append(pl.BlockSpec(memory_space=pl.ANY))
        args.append(c_out)
        out_shape += [
            jax.ShapeDtypeStruct(c_out.shape, F32),
            jax.ShapeDtypeStruct((n_seq, 2, ML_HEADS, ML_DIM), F32),
            jax.ShapeDtypeStruct((n_seq, 2, ML_HEADS, 128), F32),
        ]
        out_specs += [
            pl.BlockSpec((nb, None, 2, ML_HEADS, ML_DIM, ML_DIM), lambda i: (i, l, 0, 0, 0, 0)),
            pl.BlockSpec((nb, 2, ML_HEADS, ML_DIM), lambda i: (i, 0, 0, 0)),
            pl.BlockSpec((nb, 2, ML_HEADS, 128), lambda i: (i, 0, 0, 0)),
        ]
    n_chain = nb * ML_HEADS * 2
    return pl.pallas_call(
        functools.partial(_ml_kernel, t_len=t_len, nb=nb, unroll=2,
                          zero_init=prompt, emit_state=prompt),
        grid=(n_seq // nb,),
        in_specs=in_specs,
        out_specs=out_specs,
        out_shape=out_shape,
        input_output_aliases=aliases,
        scratch_shapes=[pltpu.VMEM((rows, 512), F32), pltpu.VMEM((rows, 128), F32),
                        pltpu.VMEM((rows // ML_LC, 16, ML_LC), F32),
                        pltpu.VMEM((n_chain * ML_DIM, 2 * ML_DIM), F32), pltpu.VMEM((n_chain, 128), F32)],
        compiler_params=_cparams(("arbitrary",), 56),
        name="mlstm_ctx" if prompt else "mlstm_lat",
    )(*args)


def _merge_kernel(x_ref, mod_ref, oda_ref, oml_ref, ona_ref, g0_ref, g1_ref, g2_ref,
                  wda_ref, wml_ref, wna_ref, wo_ref, o_ref):
    def branch(o, w, g):
        return jax.nn.sigmoid(g[...].astype(F32)) * _dot(o[...], w[...])

    merged = (branch(oda_ref, wda_ref, g0_ref) + branch(oml_ref, wml_ref, g1_ref)
              + branch(ona_ref, wna_ref, g2_ref))
    o_ref[...] = x_ref[...] + mod_ref[2:3, :] * _dot(merged.astype(BF16), wo_ref[...])


def _merge_out(x, modl, o_da, o_ml, o_na, cols, w_da, w_ml, w_na, w_out, *, prompt):
    m = x.shape[0]
    tm = 512
    const = lambda i: (0, 0)
    a, gt = cols["merge"]
    g0 = gt // 2
    return pl.pallas_call(
        _merge_kernel,
        grid=(m // tm,),
        in_specs=[
            pl.BlockSpec((tm, D_MODEL), lambda i: (i, 0)),
            pl.BlockSpec((None, 6, D_MODEL), _mod_row_map(prompt, tm)),
            pl.BlockSpec((tm, 512), lambda i: (i, 0)),
            pl.BlockSpec((tm, 512), lambda i: (i, 0)),
            pl.BlockSpec((tm, 512), lambda i: (i, 0)),
            pl.BlockSpec((tm, D_MODEL), lambda i: (i, g0)),
            pl.BlockSpec((tm, D_MODEL), lambda i: (i, g0 + 1)),
            pl.BlockSpec((tm, D_MODEL), lambda i: (i, g0 + 2)),
            pl.BlockSpec((512, D_MODEL), const),
            pl.BlockSpec((512, D_MODEL), const),
            pl.BlockSpec((512, D_MODEL), const),
            pl.BlockSpec((D_MODEL, D_MODEL), const),
        ],
        out_specs=pl.BlockSpec((tm, D_MODEL), lambda i: (i, 0)),
        out_shape=jax.ShapeDtypeStruct((m, D_MODEL), F32),
        compiler_params=_cparams(("arbitrary",), 48),
        name="merge_out_ctx" if prompt else "merge_out_lat",
    )(x, modl, o_da, o_ml, o_na, a, a, a, w_da, w_ml, w_na, w_out)


def _ffn_kernel(x_ref, mod_ref, g_ref, w1_ref, b1_ref, w2_ref, b2_ref, *refs, final):
    o_ref, h_scr, acc_scr = refs[-3:]
    f = pl.program_id(1)

    @pl.when(f == 0)
    def _():
        h_scr[...] = _modulated_norm(x_ref[...], g_ref[...], mod_ref[3:4, :], mod_ref[4:5, :]).astype(BF16)
        acc_scr[...] = jnp.zeros_like(acc_scr)

    h = h_scr[...]
    part = None
    for c in range(w1_ref.shape[1] // IP_SUB):
        cs = slice(c * IP_SUB, (c + 1) * IP_SUB)
        u = jnp.maximum(_dot(h, w1_ref[:, cs]) + b1_ref[:, cs], 0.0)
        d = _dot((u * u).astype(BF16), w2_ref[cs, :])
        part = d if part is None else part + d
    acc_scr[...] += part

    @pl.when(f == pl.num_programs(1) - 1)
    def _():
        y = x_ref[...] + mod_ref[5:6, :] * (acc_scr[...] + b2_ref[...])
        if final:
            ms = jnp.mean(y * y, axis=-1, keepdims=True)
            y = y * lax.rsqrt(ms + EPS) * refs[0][...]
        o_ref[...] = y


def _ffn(x, modl, g2, w1, b1, w2, b2, g_final, *, prompt):
    m = x.shape[0]
    tm, tf = 1024, 1024
    final = g_final is not None
    in_specs = [
        pl.BlockSpec((tm, D_MODEL), lambda i, f: (i, 0)),
        pl.BlockSpec((None, 6, D_MODEL), _mod_row_map(prompt, tm)),
        pl.BlockSpec((1, D_MODEL), lambda i, f: (0, 0)),
        pl.BlockSpec((D_MODEL, tf), lambda i, f: (0, f)),
        pl.BlockSpec((1, tf), lambda i, f: (0, f)),
        pl.BlockSpec((tf, D_MODEL), lambda i, f: (f, 0)),
        pl.BlockSpec((1, D_MODEL), lambda i, f: (0, 0)),
    ]
    args = [x, modl, g2, w1, b1, w2, b2]
    if final:
        in_specs.append(pl.BlockSpec((1, D_MODEL), lambda i, f: (0, 0)))
        args.append(g_final)
    return pl.pallas_call(
        functools.partial(_ffn_kernel, final=final),
        grid=(m // tm, D_FF // tf),
        in_specs=in_specs,
        out_specs=pl.BlockSpec((tm, D_MODEL), lambda i, f: (i, 0)),
        out_shape=jax.ShapeDtypeStruct((m, D_MODEL), F32),
        scratch_shapes=[pltpu.VMEM((tm, D_MODEL), BF16), pltpu.VMEM((tm, D_MODEL), F32)],
        compiler_params=_cparams(("arbitrary", "arbitrary"), 48),
        name="ffn_ctx" if prompt else "ffn_lat",
    )(*args)


def _rope_tables():
    t = jnp.arange(DEC_SEQ)
    lane = jnp.arange(128)
    sub = lane % 32
    freq = ROPE_BASE ** (-(2 * (sub % 16)).astype(F32) / 32.0)
    use_row = (lane % 64) < 32
    posv = jnp.where(use_row[None, :], (t // GRID_W)[:, None], (t % GRID_W)[:, None]).astype(F32)
    ang = posv * freq[None, :]
    sign = jnp.where(sub < 16, -1.0, 1.0).astype(F32)
    return jnp.cos(ang), jnp.sin(ang) * sign[None, :]


def kernel(x_prompt, x_sample, cache_da_k, cache_da_v, cache_na_k, cache_na_v, state_ml_C, state_ml_n,
           state_ml_m, c, c_ctx, w_mod, b_mod, norm1, w_in, b_in, da_lam, da_subln, ml_norm, na_rpb,
           w_up_da, w_up_ml, w_up_na, w_out, norm2, w_ff1, b_ff1, w_ff2, b_ff2, norm_f):
    xp = x_prompt.reshape(BATCH * SEQ, D_MODEL)
    xs = x_sample.reshape(DEC_BATCH * DEC_SEQ, D_MODEL)

    segs = ((512, 1024), (1024, 1536), (4112, 4624), (4624, 5136), (0, 512), (1536, GATE_OFF),
            (GATE_OFF + 16, 4112), (5136, 8208))
    w_main = jnp.concatenate([w_in[..., a:b] for a, b in segs], axis=-1).astype(BF16)
    b_main = jnp.concatenate([b_in[..., a:b] for a, b in segs], axis=-1).reshape(DEPTH, 1, N_MAIN)
    tiles = ("da_k", "da_v", "na_k", "na_v", "da_q", "ml_q", "ml_k", "ml_v", "ml_o", "na_q", "merge")
    tile_of = {n: t for t, n in enumerate(tiles)}
    src, dst = [], []
    for hh in range(ML_HEADS):
        for dr in range(2):
            for tt in range(2):
                src.append(GATE_OFF + dr * 2 * ML_HEADS + tt * ML_HEADS + hh)
                dst.append(hh * 4 + dr * 2 + tt)
    src, dst = np.array(src), np.array(dst)
    w_g = jnp.zeros((DEPTH, D_MODEL, 128), F32).at[:, :, dst].set(w_in[:, :, src]).astype(BF16)
    b_g = jnp.zeros((DEPTH, 1, 128), F32).at[:, 0, dst].set(b_in[:, src])
    cs = np.ones((1, N_MAIN), np.float32)
    for name, scale in (("da_q", DA_QK ** -0.5), ("ml_k", ML_DIM ** -0.5), ("na_q", NA_DIM ** -0.5)):
        cs[0, tile_of[name] * 512:(tile_of[name] + 1) * 512] = scale
    cscale = jnp.asarray(cs)
    w_da_b, w_ml_b, w_na_b, w_out_b = (w.astype(BF16) for w in (w_up_da, w_up_ml, w_up_na, w_out))
    w_ff1_b, w_ff2_b = w_ff1.astype(BF16), w_ff2.astype(BF16)
    rope_tabs = _rope_tables()

    cc = jnp.zeros((8, D_MODEL), F32).at[0].set(c_ctx).at[1:1 + DEC_BATCH].set(c)
    mod = _modulation(cc, w_mod, b_mod).reshape(DEPTH, 8, 6, D_MODEL)

    cdk = cache_da_k.reshape(DEC_BATCH, DEPTH, PAST_LEN, 512)
    cdv = cache_da_v.reshape(DEC_BATCH, DEPTH, PAST_LEN, 512)
    cnk = cache_na_k.reshape(DEC_BATCH, DEPTH, PAST_LEN, 512)
    cnv = cache_na_v.reshape(DEC_BATCH, DEPTH, PAST_LEN, 512)

    kv_outs = [jnp.zeros((BATCH, DEPTH, SEQ, 512), F32) for _ in range(4)]
    c_out = jnp.zeros((BATCH, DEPTH, 2, ML_HEADS, ML_DIM, ML_DIM), F32)
    coll_n, coll_m = [], []
    gf = norm_f.reshape(1, D_MODEL)
    for l in range(DEPTH):
        lam_init = 0.8 - 0.6 * math.exp(-0.3 * l)
        modl = mod[l]
        g1 = norm1[l].reshape(1, D_MODEL)
        g2 = norm2[l].reshape(1, D_MODEL)
        sub_row = da_subln[l].reshape(1, 128)
        bias_tab = _na_bias_table(na_rpb[l])
        for prompt in (True, False):
            x = xp if prompt else xs
            if prompt:
                a_kv, kv_outs = _in_proj_kv(x, modl, g1, w_main[l], b_main[l], kv_outs, l)
                a, gates = _in_proj(x, modl, g1, w_main[l], b_main[l], cscale, w_g[l], b_g[l],
                                    col0=N_KV, prompt=True)
                cols = {n: (a_kv, t) if t < 4 else (a, t - 4) for n, t in tile_of.items()}
            else:
                a, gates = _in_proj(x, modl, g1, w_main[l], b_main[l], cscale, w_g[l], b_g[l],
                                    col0=0, prompt=False)
                cols = {n: (a, t) for n, t in tile_of.items()}
            o_da = _diff_attention(cols, cdk, cdv, rope_tabs, l, da_lam[l], sub_row, lam_init, prompt=prompt)
            ml_out = _mlstm(cols, gates, (state_ml_C, state_ml_n, state_ml_m), c_out, l, ml_norm[l],
                            prompt=prompt)
            o_ml = ml_out[0]
            if prompt:
                c_out = ml_out[1]
                coll_n.append(ml_out[2])
                coll_m.append(ml_out[3][..., 0])
                o_na = _na_ctx(cols)
            else:
                o_na = _na_lat(cols, cnk, cnv, l, bias_tab)
            x = _merge_out(x, modl, o_da, o_ml, o_na, cols, w_da_b[l], w_ml_b[l], w_na_b[l], w_out_b[l],
                           prompt=prompt)
            x = _ffn(x, modl, g2, w_ff1_b[l], b_ff1[l].reshape(1, D_FF), w_ff2_b[l],
                     b_ff2[l].reshape(1, D_MODEL), gf if l == DEPTH - 1 else None, prompt=prompt)
            if prompt:
                xp = x
            else:
                xs = x

    y_prompt = xp.reshape(BATCH, SEQ, D_MODEL)
    y_sample = xs.reshape(DEC_BATCH, DEC_SEQ, D_MODEL)
    new_da_k = kv_outs[0].reshape(BATCH, DEPTH, SEQ, DA_HEADS, 2 * DA_QK)
    new_da_v = kv_outs[1].reshape(BATCH, DEPTH, SEQ, DA_HEADS, 2 * DA_QK)
    new_na_k = kv_outs[2].reshape(BATCH, DEPTH, SEQ, NA_HEADS, NA_DIM)
    new_na_v = kv_outs[3].reshape(BATCH, DEPTH, SEQ, NA_HEADS, NA_DIM)
    new_ml_n = jnp.stack(coll_n, axis=1)
    new_ml_m = jnp.stack(coll_m, axis=1)
    return (y_prompt, y_sample, new_da_k, new_da_v, new_na_k, new_na_v, c_out, new_ml_n, new_ml_m)
```

```python
import functools
import math

import jax
import jax.numpy as jnp
import numpy as np
from jax import lax
from jax.experimental import pallas as pl
from jax.experimental.pallas import tpu as pltpu

F32 = jnp.float32
BF16 = jnp.bfloat16
I32 = jnp.int32

D_MODEL = 1024
BATCH = 32
SEQ = 256
DEPTH = 4
DEC_BATCH = 2
DEC_SEQ = 4096
PAST_LEN = 512
GRID_W = 64
DA_HEADS = 4
DA_QK = 64
ML_HEADS = 4
ML_DIM = 128
ML_LC = 256
NA_HEADS = 8
NA_DIM = 64
NA_WIN_ROWS = 8
NA_WIN_COLS = 16
D_FF = 4 * D_MODEL
ROPE_BASE = 10000.0
EPS = 1e-6
N_MAIN = 8192
GATE_OFF = 3584
NEG = -1e30

MIB = 1024 * 1024
NT_DIMS = (((1,), (1,)), ((), ()))


def _cparams(sem, vmem_mib):
    return pltpu.CompilerParams(dimension_semantics=sem, vmem_limit_bytes=vmem_mib * MIB)


def _dot(a, b):
    return jnp.dot(a, b, preferred_element_type=F32)


def _dot_nt(a, b):
    return lax.dot_general(a, b, NT_DIMS, preferred_element_type=F32)


def _mod_kernel(c_ref, w_ref, b_ref, o_ref):
    c = c_ref[...]
    s = (c * jax.nn.sigmoid(c)).astype(BF16)
    o_ref[0] = _dot(s, w_ref[0].astype(BF16)) + b_ref[0]


def _modulation(cc, w_mod, b_mod):
    tn = 1536
    n = 6 * D_MODEL
    return pl.pallas_call(
        _mod_kernel,
        grid=(DEPTH, n // tn),
        in_specs=[
            pl.BlockSpec((8, D_MODEL), lambda l, j: (0, 0)),
            pl.BlockSpec((1, D_MODEL, tn), lambda l, j: (l, 0, j)),
            pl.BlockSpec((1, 1, tn), lambda l, j: (l, 0, j)),
        ],
        out_specs=pl.BlockSpec((1, 8, tn), lambda l, j: (l, 0, j)),
        out_shape=jax.ShapeDtypeStruct((DEPTH, 8, n), F32),
        compiler_params=_cparams(("arbitrary", "arbitrary"), 40),
        name="modulation",
    )(cc, w_mod, b_mod.reshape(DEPTH, 1, n))


def _modulated_norm(x, g, shift, scale):
    ms = jnp.mean(x * x, axis=-1, keepdims=True)
    y = x * lax.rsqrt(ms + EPS) * g
    return y * (1.0 + scale) + shift


IP_TM = 1024
IP_SUB = 512
N_KV = 2048


def _inproj_kernel(x_ref, mod_ref, g_ref, w_ref, b_ref, cs_ref, wg_ref, bg_ref, a_ref, gate_ref, h_scr):
    @pl.when(pl.program_id(1) == 0)
    def _():
        h = _modulated_norm(x_ref[...], g_ref[...], mod_ref[0:1, :], mod_ref[1:2, :]).astype(BF16)
        h_scr[...] = h
        gate_ref[...] = _dot(h, wg_ref[...]) + bg_ref[...]

    h = h_scr[...]
    for c in range(a_ref.shape[1] // IP_SUB):
        cs = slice(c * IP_SUB, (c + 1) * IP_SUB)
        a_ref[:, cs] = ((_dot(h, w_ref[:, cs]) + b_ref[:, cs]) * cs_ref[:, cs]).astype(BF16)


def _inproj_kv_kernel(x_ref, mod_ref, g_ref, w_ref, b_ref, *refs):
    a_ref = refs[4]
    f32_refs = refs[5:9]
    h = _modulated_norm(x_ref[...], g_ref[...], mod_ref[0:1, :], mod_ref[1:2, :]).astype(BF16)
    for c, ref in enumerate(f32_refs):
        cs = slice(c * IP_SUB, (c + 1) * IP_SUB)
        acc = _dot(h, w_ref[:, cs]) + b_ref[:, cs]
        ref[...] = acc.reshape(ref.shape)
        a_ref[:, cs] = acc.astype(BF16)


def _mod_row_map(prompt, tm):
    if prompt:
        return lambda i, *_: (0, 0, 0)
    return lambda i, *_: (1 + i // (DEC_SEQ // tm), 0, 0)


def _in_proj(x, modl, g1, w_main, b_main, cscale, w_g, b_g, *, col0, prompt):
    m = x.shape[0]
    tm, tn = IP_TM, 1024
    j0 = col0 // tn
    return pl.pallas_call(
        _inproj_kernel,
        grid=(m // tm, (N_MAIN - col0) // tn),
        in_specs=[
            pl.BlockSpec((tm, D_MODEL), lambda i, j: (i, 0)),
            pl.BlockSpec((None, 6, D_MODEL), _mod_row_map(prompt, tm)),
            pl.BlockSpec((1, D_MODEL), lambda i, j: (0, 0)),
            pl.BlockSpec((D_MODEL, tn), lambda i, j: (0, j + j0)),
            pl.BlockSpec((1, tn), lambda i, j: (0, j + j0)),
            pl.BlockSpec((1, tn), lambda i, j: (0, j + j0)),
            pl.BlockSpec((D_MODEL, 128), lambda i, j: (0, 0)),
            pl.BlockSpec((1, 128), lambda i, j: (0, 0)),
        ],
        out_specs=[pl.BlockSpec((tm, tn), lambda i, j: (i, j)),
                   pl.BlockSpec((tm, 128), lambda i, j: (i, 0))],
        out_shape=[jax.ShapeDtypeStruct((m, N_MAIN - col0), BF16), jax.ShapeDtypeStruct((m, 128), F32)],
        scratch_shapes=[pltpu.VMEM((tm, D_MODEL), BF16)],
        compiler_params=_cparams(("arbitrary", "arbitrary"), 48),
        name="in_proj_ctx" if prompt else "in_proj_lat",
    )(x, modl, g1, w_main, b_main, cscale, w_g, b_g)


def _in_proj_kv(x, modl, g1, w_main, b_main, kv_outs, l):
    m = x.shape[0]
    tm = IP_TM
    nbat = tm // SEQ
    kv_spec = pl.BlockSpec((nbat, None, SEQ, IP_SUB), lambda i: (i, l, 0, 0))
    outs = pl.pallas_call(
        _inproj_kv_kernel,
        grid=(m // tm,),
        in_specs=[
            pl.BlockSpec((tm, D_MODEL), lambda i: (i, 0)),
            pl.BlockSpec((None, 6, D_MODEL), _mod_row_map(True, tm)),
            pl.BlockSpec((1, D_MODEL), lambda i: (0, 0)),
            pl.BlockSpec((D_MODEL, N_KV), lambda i: (0, 0)),
            pl.BlockSpec((1, N_KV), lambda i: (0, 0)),
        ] + [pl.BlockSpec(memory_space=pl.ANY)] * 4,
        out_specs=[pl.BlockSpec((tm, N_KV), lambda i: (i, 0))] + [kv_spec] * 4,
        out_shape=[jax.ShapeDtypeStruct((m, N_KV), BF16)]
                  + [jax.ShapeDtypeStruct(o.shape, o.dtype) for o in kv_outs],
        input_output_aliases={5 + c: 1 + c for c in range(4)},
        compiler_params=_cparams(("arbitrary",), 56),
        name="in_proj_kv_ctx",
    )(x, modl, g1, w_main, b_main, *kv_outs)
    return outs[0], outs[1:]


def _half_masked(q, upper):
    lane = lax.broadcasted_iota(I32, q.shape, 1)
    keep = (lane >= 64) if upper else (lane < 64)
    return jnp.where(keep, q, jnp.zeros_like(q))


def _da_block(qm, k_t, va, m_old, acc_old):
    return _sm_block(_dot(qm, k_t), va, m_old, acc_old)


def _sm_block(s, va, m_old, acc_old):
    nl = s.shape[1] // 128
    mx = s[:, 0:128]
    for c in range(1, nl):
        mx = jnp.maximum(mx, s[:, c * 128:(c + 1) * 128])
    m_new = jnp.broadcast_to(mx.max(axis=1, keepdims=True), mx.shape)
    if m_old is not None:
        m_new = jnp.maximum(m_old, m_new)
    e = jnp.concatenate([jnp.exp(s[:, c * 128:(c + 1) * 128] - m_new) for c in range(nl)], axis=1)
    pv = _dot(e.astype(BF16), va)
    if m_old is None:
        return m_new, pv
    alpha = jnp.exp(m_old - m_new)
    return m_new, acc_old * jnp.concatenate([alpha, alpha], axis=1) + pv


def _rope(x, cos, sin):
    lane = lax.broadcasted_iota(I32, x.shape, 1)
    partner = jnp.where((lane & 16) == 0, pltpu.roll(x, 112, 1), pltpu.roll(x, 16, 1))
    return x * cos + partner * sin


def _da_kernel(*refs, hp, s_new, s_cache, kb, rope, lam_init):
    q_ref, k_ref, v_ref = refs[:3]
    pos = 3
    if s_cache:
        ck_ref, cv_ref = refs[pos:pos + 2]
        pos += 2
    if rope:
        cos_ref, sin_ref = refs[pos:pos + 2]
        pos += 2
    lam_ref, sub_ref, o_ref, kt_scr, va_scr, m_scr, acc_scr = refs[pos:pos + 7]
    qi = pl.program_id(2)
    tq = q_ref.shape[0]
    n_new, n_all = s_new // kb, (s_new + s_cache) // kb

    @pl.when(qi == 0)
    def _():
        for hh in range(hp):
            sl = slice(hh * 128, (hh + 1) * 128)
            for j in range(n_new):
                rows = slice(j * kb, (j + 1) * kb)
                kblk = k_ref[rows, sl].astype(F32)
                if rope:
                    kblk = _rope(kblk, cos_ref[rows, :], sin_ref[rows, :])
                kt_scr[hh, j] = kblk.T.astype(BF16)
            va_scr[hh, 0:s_new, 0:128] = v_ref[:, sl]
            for j in range(n_all - n_new):
                kt_scr[hh, n_new + j] = ck_ref[j * kb:(j + 1) * kb, sl].T.astype(BF16)
            if s_cache:
                va_scr[hh, s_new:s_new + s_cache, 0:128] = cv_ref[:, sl].astype(BF16)
            va_scr[hh, :, 128:256] = jnp.ones((s_new + s_cache, 128), BF16)

    lv = lam_ref[...]
    lam = (jnp.exp(jnp.sum(lv[0:1] * lv[1:2], axis=1, keepdims=True))
           - jnp.exp(jnp.sum(lv[2:3] * lv[3:4], axis=1, keepdims=True)) + lam_init)
    for hh in range(hp):
        sl = slice(hh * 128, (hh + 1) * 128)
        q = q_ref[:, sl]
        if rope:
            q_rows = pl.ds(pl.multiple_of(qi * tq, tq), tq)
            q = _rope(q.astype(F32), cos_ref[q_rows, :], sin_ref[q_rows, :]).astype(BF16)
        qms = (_half_masked(q, False), _half_masked(q, True))
        if n_all == 1:
            accs = [_da_block(qm, kt_scr[hh, 0], va_scr[hh], None, None)[1] for qm in qms]
        else:
            m_scr[...] = jnp.full(m_scr.shape, NEG, F32)
            acc_scr[...] = jnp.zeros_like(acc_scr)

            def body(j, carry, hh=hh, qms=qms):
                va = va_scr[hh, pl.ds(pl.multiple_of(j * kb, kb), kb), :]
                k_t = kt_scr[hh, j]
                new = [_da_block(qm, k_t, va, m_scr[mp], acc_scr[mp]) for mp, qm in enumerate(qms)]
                for mp, (m_new, acc) in enumerate(new):
                    m_scr[mp] = m_new
                    acc_scr[mp] = acc
                return carry

            lax.fori_loop(0, n_all, body, 0, unroll=True)
            accs = [acc_scr[0], acc_scr[1]]
        o1 = accs[0][:, :128] / accs[0][:, 128:]
        o2 = accs[1][:, :128] / accs[1][:, 128:]
        o = o1 - lam * o2
        ms = jnp.mean(o * o, axis=1, keepdims=True)
        on = o * lax.rsqrt(ms + EPS) * sub_ref[...] * (1.0 - lam_init)
        o_ref[:, sl] = on.astype(BF16)


def _diff_attention(cols, cache_k, cache_v, rope_tabs, l, da_lam_l, subln_row, lam_init, *, prompt):
    (qa, qt), (ka, kt), (va, vt) = cols["da_q"], cols["da_k"], cols["da_v"]
    m = qa.shape[0]
    if prompt:
        hp, tq, s_new, s_cache, kb = DA_HEADS, SEQ, SEQ, 0, SEQ
        grid = (BATCH, 1, 1)
        in_specs = [
            pl.BlockSpec((SEQ, 512), lambda b, h, qi: (b, qt)),
            pl.BlockSpec((SEQ, 512), lambda b, h, qi: (b, kt)),
            pl.BlockSpec((SEQ, 512), lambda b, h, qi: (b, vt)),
        ]
        args = [qa, ka, va]
        out_spec = pl.BlockSpec((SEQ, 512), lambda b, h, qi: (b, 0))
    else:
        hp, tq, s_new, s_cache, kb = 1, 1024, DEC_SEQ, PAST_LEN, 512
        nq = DEC_SEQ // tq
        grid = (DEC_BATCH, DA_HEADS, nq)
        in_specs = [
            pl.BlockSpec((tq, 128), lambda b, h, qi: (b * nq + qi, 4 * qt + h)),
            pl.BlockSpec((DEC_SEQ, 128), lambda b, h, qi: (b, 4 * kt + h)),
            pl.BlockSpec((DEC_SEQ, 128), lambda b, h, qi: (b, 4 * vt + h)),
            pl.BlockSpec((None, None, PAST_LEN, 128), lambda b, h, qi: (b, l, 0, h)),
            pl.BlockSpec((None, None, PAST_LEN, 128), lambda b, h, qi: (b, l, 0, h)),
            pl.BlockSpec((DEC_SEQ, 128), lambda b, h, qi: (0, 0)),
            pl.BlockSpec((DEC_SEQ, 128), lambda b, h, qi: (0, 0)),
        ]
        args = [qa, ka, va, cache_k, cache_v, *rope_tabs]
        out_spec = pl.BlockSpec((tq, 128), lambda b, h, qi: (b * nq + qi, h))
    in_specs += [
        pl.BlockSpec((4, DA_QK), lambda b, h, qi: (0, 0)),
        pl.BlockSpec((1, 128), lambda b, h, qi: (0, 0)),
    ]
    args += [da_lam_l, subln_row]
    s_all = s_new + s_cache
    return pl.pallas_call(
        functools.partial(_da_kernel, hp=hp, s_new=s_new, s_cache=s_cache, kb=kb, rope=not prompt,
                          lam_init=lam_init),
        grid=grid,
        in_specs=in_specs,
        out_specs=out_spec,
        out_shape=jax.ShapeDtypeStruct((m, 512), BF16),
        scratch_shapes=[pltpu.VMEM((hp, s_all // kb, 128, kb), BF16), pltpu.VMEM((hp, s_all, 256), BF16),
                        pltpu.VMEM((2, tq, 128), F32), pltpu.VMEM((2, tq, 256), F32)],
        compiler_params=_cparams(("arbitrary", "arbitrary", "arbitrary"), 48),
        name="diff_attn_ctx" if prompt else "diff_attn_lat",
    )(*args)


def _na_bias_kernel(rpb_ref, o_ref):
    h = pl.program_id(0)
    n_dc = 2 * NA_WIN_COLS - 1
    n_dr = 2 * NA_WIN_ROWS - 1
    qc = lax.broadcasted_iota(I32, (GRID_W, 128), 0)
    lane = lax.broadcasted_iota(I32, (GRID_W, 128), 1)
    kc = lane & (GRID_W - 1)
    dcm = kc - qc + (NA_WIN_COLS - 1)
    cstart = jnp.clip(qc - NA_WIN_COLS // 2, 0, GRID_W - NA_WIN_COLS)
    left = lane < GRID_W
    base = h * (n_dr * n_dc)
    for e in range(16):
        acc = jnp.zeros((GRID_W, 128), F32)
        for dc in range(n_dc):
            lv = rpb_ref[base + (e - 1) * n_dc + dc] if e >= 1 else 0.0
            rv = rpb_ref[base + e * n_dc + dc] if e < n_dr else 0.0
            acc = jnp.where(dcm == dc, jnp.where(left, lv, rv), acc)
        o_ref[0, e] = jnp.where(kc >= cstart, jnp.where(kc < cstart + NA_WIN_COLS, acc, NEG), NEG)


def _na_bias_table(rpb_l):
    return pl.pallas_call(
        _na_bias_kernel,
        grid=(NA_HEADS,),
        in_specs=[pl.BlockSpec(memory_space=pltpu.SMEM)],
        out_specs=pl.BlockSpec((1, 16, GRID_W, 128), lambda h: (h, 0, 0, 0)),
        out_shape=jax.ShapeDtypeStruct((NA_HEADS, 16, GRID_W, 128), F32),
        compiler_params=_cparams(("arbitrary",), 16),
        name="na_bias_table",
    )(rpb_l.reshape(-1))


def _na_lat_kernel(q_ref, k_ref, v_ref, ck_ref, cv_ref, bp_ref, o_ref):
    rb = pl.program_id(1)
    ws = jnp.clip(2 * rb - 1, 0, 12)
    delta = 4 * ws - 8 * rb
    tok0 = pl.multiple_of(ws * 256, 256)
    qr = 8 * rb + (lax.broadcasted_iota(I32, (512, 1024), 0) >> 6)
    kr = 4 * ws + (lax.broadcasted_iota(I32, (512, 1024), 1) >> 6)
    st = jnp.clip(qr - NA_WIN_ROWS // 2, 0, GRID_W - NA_WIN_ROWS)
    rowmask = jnp.where(kr >= st, jnp.where(kr < st + NA_WIN_ROWS, 0.0, NEG), NEG)
    ones_b = jnp.ones((1024 + PAST_LEN, 128), BF16)
    lane = lax.broadcasted_iota(I32, (512, 128), 1)
    for g in range(NA_HEADS // 2):
        sl = slice(g * 128, (g + 1) * 128)
        q2 = q_ref[:, sl]
        kall = jnp.concatenate([k_ref[pl.ds(tok0, 1024), sl].astype(F32), ck_ref[:, sl]], axis=0)
        k_t = kall.T.astype(BF16)
        vall = jnp.concatenate([v_ref[pl.ds(tok0, 1024), sl], cv_ref[:, sl].astype(BF16)], axis=0)
        va = jnp.concatenate([vall, ones_b], axis=1)
        outs = []
        for par in range(2):
            h = 2 * g + par
            s = _dot(_half_masked(q2, par == 1), k_t)
            rows = []
            for qrl in range(8):
                tiles = [bp_ref[h, jnp.clip(delta + 2 * jk - qrl + 8, 0, 15)] for jk in range(8)]
                rows.append(jnp.concatenate(tiles, axis=1))
            bias = jnp.concatenate(rows, axis=0)
            s = jnp.concatenate([s[:, :1024] + bias + rowmask, s[:, 1024:]], axis=1)
            acc = _sm_block(s, va, None, None)[1]
            outs.append(acc[:, :128] / acc[:, 128:])
        o_ref[:, sl] = jnp.where(lane < 64, outs[0], outs[1]).astype(BF16)


def _na_lat(cols, cache_k, cache_v, l, bias_tab):
    (qa, qt), (ka, kt), (va, vt) = cols["na_q"], cols["na_k"], cols["na_v"]
    return pl.pallas_call(
        _na_lat_kernel,
        grid=(DEC_BATCH, 8),
        in_specs=[
            pl.BlockSpec((512, 512), lambda b, rb: (b * 8 + rb, qt)),
            pl.BlockSpec((DEC_SEQ, 512), lambda b, rb: (b, kt)),
            pl.BlockSpec((DEC_SEQ, 512), lambda b, rb: (b, vt)),
            pl.BlockSpec((None, None, PAST_LEN, 512), lambda b, rb: (b, l, 0, 0)),
            pl.BlockSpec((None, None, PAST_LEN, 512), lambda b, rb: (b, l, 0, 0)),
            pl.BlockSpec((NA_HEADS, 16, GRID_W, 128), lambda b, rb: (0, 0, 0, 0)),
        ],
        out_specs=pl.BlockSpec((512, 512), lambda b, rb: (b * 8 + rb, 0)),
        out_shape=jax.ShapeDtypeStruct((DEC_BATCH * DEC_SEQ, 512), BF16),
        compiler_params=_cparams(("arbitrary", "arbitrary"), 56),
        name="nbr_attn_lat",
    )(qa, ka, va, cache_k, cache_v, bias_tab)


def _na_ctx_kernel(q_ref, k_ref, v_ref, o_ref):
    ones_b = jnp.ones((q_ref.shape[0], 128), BF16)
    lane = lax.broadcasted_iota(I32, (q_ref.shape[0], 128), 1)
    for g in range(NA_HEADS // 2):
        sl = slice(g * 128, (g + 1) * 128)
        q2 = q_ref[:, sl]
        k_t = k_ref[:, sl].astype(F32).T.astype(BF16)
        va = jnp.concatenate([v_ref[:, sl], ones_b], axis=1)
        acc_a = _da_block(_half_masked(q2, False), k_t, va, None, None)[1]
        acc_b = _da_block(_half_masked(q2, True), k_t, va, None, None)[1]
        o_pair = jnp.where(lane < 64, acc_a[:, :128] / acc_a[:, 128:], acc_b[:, :128] / acc_b[:, 128:])
        o_ref[:, sl] = o_pair.astype(BF16)


def _na_ctx(cols):
    (qa, qt), (ka, kt), (va, vt) = cols["na_q"], cols["na_k"], cols["na_v"]
    return pl.pallas_call(
        _na_ctx_kernel,
        grid=(BATCH,),
        in_specs=[
            pl.BlockSpec((SEQ, 512), lambda b: (b, qt)),
            pl.BlockSpec((SEQ, 512), lambda b: (b, kt)),
            pl.BlockSpec((SEQ, 512), lambda b: (b, vt)),
        ],
        out_specs=pl.BlockSpec((SEQ, 512), lambda b: (b, 0)),
        out_shape=jax.ShapeDtypeStruct((BATCH * SEQ, 512), BF16),
        compiler_params=_cparams(("arbitrary",), 32),
        name="nbr_attn_ctx",
    )(qa, ka, va)


def _log_sigmoid(x):
    return jnp.minimum(x, 0.0) - jnp.log1p(jnp.exp(-jnp.abs(x)))


def _split3(x):
    x1 = x.astype(BF16)
    r1 = x - x1.astype(F32)
    x2 = r1.astype(BF16)
    x3 = (r1 - x2.astype(F32)).astype(BF16)
    return x1, x2, x3


def _ml_kernel(*refs, t_len, nb, unroll, zero_init, emit_state):
    q_ref, k_ref, v_ref, og_ref, g_ref = refs[:5]
    pos = 5
    if not zero_init:
        c0_ref, n0_ref, m0_ref = refs[pos:pos + 3]
        pos += 3
    nrm_ref = refs[pos]
    pos += 2 if emit_state else 1
    o_ref = refs[pos]
    pos += 1
    if emit_state:
        cst_ref, nst_ref, mst_ref = refs[pos:pos + 3]
        pos += 3
    hsum_scr, colb_scr, cmb_scr, rowt_scr, s_scr, m_scr = refs[pos:pos + 6]
    step = pl.program_id(0)
    lc = ML_LC
    nc = t_len // lc
    chains = [(bi, h, d) for bi in range(nb) for h in range(ML_HEADS) for d in range(2)]

    ri = lax.broadcasted_iota(I32, (lc, lc), 0)
    ci = lax.broadcasted_iota(I32, (lc, lc), 1)
    lo = ri >= ci
    up = ri <= ci
    lo_b = jnp.where(lo, 1.0, 0.0).astype(BF16)
    up_b = jnp.where(up, 1.0, 0.0).astype(BF16)
    lane = lax.broadcasted_iota(I32, (lc, 128), 1)
    trow = lax.broadcasted_iota(I32, (lc, 128), 0)
    is_forget = (lane & 1) == 1
    is_bwd = (lane & 2) == 2

    def tri_left(tri, x):
        return sum(_dot(tri, p) for p in _split3(x))

    def prep(c, carry):
        r0 = pl.multiple_of(c * lc, lc)
        g = g_ref[pl.ds(r0, lc), :]
        lf = _log_sigmoid(g)
        cb = jnp.where(is_forget, jnp.where(is_bwd, tri_left(up_b, lf), tri_left(lo_b, lf)), g)
        colb_scr[pl.ds(r0, lc), :] = cb
        ar = pltpu.roll(cb, 127, 1) - cb
        rowt_scr[c] = ar.T[0:16]
        pre = suf = -ar
        k = 1
        while k < lc:
            pre = jnp.maximum(pre, jnp.where(trow >= k, pltpu.roll(pre, k, 0), -jnp.inf))
            suf = jnp.maximum(suf, jnp.where(trow < lc - k, pltpu.roll(suf, lc - k, 0), -jnp.inf))
            k *= 2
        cmb_scr[pl.ds(r0, lc), :] = jnp.where(is_bwd, suf, pre)
        return carry

    lax.fori_loop(0, nb * nc, prep, 0, unroll=2)

    hsum_scr[...] = jnp.zeros_like(hsum_scr)
    if zero_init:
        s_scr[...] = jnp.zeros_like(s_scr)
        m_scr[...] = jnp.zeros_like(m_scr)
    else:
        for idx, (bi, h, d) in enumerate(chains):
            n_rep = jnp.broadcast_to(n0_ref[d, h:h + 1, :], (ML_DIM, ML_DIM)).T
            s_scr[idx * ML_DIM:(idx + 1) * ML_DIM, :] = jnp.concatenate([c0_ref[d, h], n_rep], axis=1)
            m0 = m0_ref[((step * nb + bi) * 2 + d) * ML_HEADS + h]
            m_scr[idx:idx + 1, :] = jnp.full((1, 128), m0, F32)

    ones_b = jnp.ones((lc, ML_DIM), BF16)
    n_ch = len(chains)

    def body(i, carry):
        cs = [i if d == 0 else nc - 1 - i for (_, _, d) in chains]
        r0s = [pl.multiple_of(bi * t_len + c * lc, lc) for (bi, _, _), c in zip(chains, cs)]
        hsl = [slice(h * 128, (h + 1) * 128) for (_, h, _) in chains]
        qs = [q_ref[pl.ds(r0, lc), sl] for r0, sl in zip(r0s, hsl)]
        ks = [k_ref[pl.ds(r0, lc), sl] for r0, sl in zip(r0s, hsl)]
        vs = [v_ref[pl.ds(r0, lc), sl] for r0, sl in zip(r0s, hsl)]
        s_old = [s_scr[idx * ML_DIM:(idx + 1) * ML_DIM, :] for idx in range(n_ch)]
        qk = [_dot_nt(q, k) for q, k in zip(qs, ks)]

        lhs, mts, bcs, ics, mms = [], [], [], [], []
        for idx, (bi, h, d) in enumerate(chains):
            l0 = h * 4 + d * 2
            cb = colb_scr[pl.ds(r0s[idx], lc), :]
            ic = cb[:, l0:l0 + 1]
            bc = cb[:, l0 + 1:l0 + 2]
            arow = rowt_scr[bi * nc + cs[idx], l0:l0 + 1, :]
            mm = m_scr[idx:idx + 1, 0:1]
            cm = cmb_scr[pl.ds(r0s[idx], lc), :][:, l0:l0 + 1]
            gmax = jnp.maximum(mm, cm)
            mt = bc + gmax
            sc = qk[idx] * jnp.exp(jnp.where(lo if d == 0 else up, -(gmax + arow), -jnp.inf))
            inter = jnp.exp(mm - gmax)
            lhs.append(jnp.concatenate([(inter * qs[idx].astype(F32)).astype(BF16), sc.astype(BF16)], axis=1))
            mts.append(mt)
            bcs.append(bc)
            ics.append(ic)
            mms.append(mm)

        nds = [_dot(a, jnp.concatenate([so.astype(BF16), jnp.concatenate([v, ones_b], axis=1)], axis=0))
               for a, so, v in zip(lhs, s_old, vs)]

        hcs, wvs, decays, m_news = [], [], [], []
        for idx, (bi, h, d) in enumerate(chains):
            last = lc - 1 if d == 0 else 0
            nd = nds[idx]
            hcs.append(nd[:, :ML_DIM] / jnp.maximum(jnp.abs(nd[:, ML_DIM:]), jnp.exp(-mts[idx])))
            m_new = mts[idx][last:last + 1, :]
            bl = bcs[idx][last:last + 1, :]
            w = jnp.exp(bl - bcs[idx] + ics[idx] - m_new)
            decays.append(jnp.exp(bl + mms[idx] - m_new))
            m_news.append(jnp.broadcast_to(m_new, (1, 128)))
            wvs.append(jnp.concatenate([(w * vs[idx].astype(F32)).astype(BF16),
                                        jnp.broadcast_to(w, (lc, ML_DIM)).astype(BF16)], axis=1))
        kts = [k.astype(F32).T.astype(BF16) for k in ks]
        kv = [_dot(kt, wv) for kt, wv in zip(kts, wvs)]
        s_scr[...] = jnp.concatenate([dc * so + x for dc, so, x in zip(decays, s_old, kv)], axis=0)
        m_scr[...] = jnp.concatenate(m_news, axis=0)
        for bi in range(nb):
            for d in range(2):
                sel = [idx for idx, ch in enumerate(chains) if ch[0] == bi and ch[2] == d]
                hsum_scr[pl.ds(r0s[sel[0]], lc), :] += jnp.concatenate([hcs[idx] for idx in sel], axis=1)
        return carry

    lax.fori_loop(0, nc, body, 0, unroll=unroll)

    for bi in range(nb):
        rows = slice(bi * t_len, (bi + 1) * t_len)
        for h in range(ML_HEADS):
            sl = slice(h * 128, (h + 1) * 128)
            hs = hsum_scr[rows, sl]
            ms = jnp.mean(hs * hs, axis=-1, keepdims=True)
            y = hs * lax.rsqrt(ms + EPS) * nrm_ref[:, sl]
            o_ref[rows, sl] = (y * jax.nn.sigmoid(og_ref[rows, sl].astype(F32))).astype(BF16)
    if emit_state:
        for idx, (bi, h, d) in enumerate(chains):
            st = s_scr[idx * ML_DIM:(idx + 1) * ML_DIM, :]
            cst_ref[bi, d, h] = st[:, :ML_DIM]
            nst_ref[bi, d, h:h + 1, :] = st[:, ML_DIM:].T[0:1, :]
            mst_ref[bi, d, h:h + 1, :] = m_scr[idx:idx + 1, :]


def _mlstm(cols, gates, states, c_out, l, ml_norm_l, *, prompt):
    names = ("ml_q", "ml_k", "ml_v", "ml_o")
    m = cols["ml_q"][0].shape[0]
    t_len = SEQ if prompt else DEC_SEQ
    nb = 2 if prompt else 1
    n_seq = m // t_len
    rows = nb * t_len
    big = {} if prompt else dict(pipeline_mode=pl.Buffered(1))
    in_specs = [pl.BlockSpec((rows, 512), functools.partial(lambda i, t: (i, t), t=cols[n][1]), **big)
                for n in names]
    in_specs.append(pl.BlockSpec((rows, 128), lambda i: (i, 0)))
    args = [cols[n][0] for n in names] + [gates]
    if not prompt:
        c0, n0, m0 = states
        in_specs += [
            pl.BlockSpec((None, None, 2, ML_HEADS, ML_DIM, ML_DIM), lambda i: (i, l, 0, 0, 0, 0)),
            pl.BlockSpec((None, None, 2, ML_HEADS, ML_DIM), lambda i: (i, l, 0, 0, 0)),
            pl.BlockSpec(memory_space=pltpu.SMEM),
        ]
        args += [c0, n0, m0[:, l].reshape(-1)]
    in_specs.append(pl.BlockSpec((1, 512), lambda i: (0, 0)))
    args.append(ml_norm_l.reshape(1, 512))
    out_shape = [jax.ShapeDtypeStruct((m, 512), BF16)]
    out_specs = [pl.BlockSpec((rows, 512), lambda i: (i, 0))]
    aliases = {}
    if prompt:
        aliases = {len(args): 1}
        in_specs.append(pl.BlockSpec(memory_space=pl.ANY))
        args.append(c_out)
        out_shape += [
            jax.ShapeDtypeStruct(c_out.shape, F32),
            jax.ShapeDtypeStruct((n_seq, 2, ML_HEADS, ML_DIM), F32),
            jax.ShapeDtypeStruct((n_seq, 2, ML_HEADS, 128), F32),
        ]
        out_specs += [
            pl.BlockSpec((nb, None, 2, ML_HEADS, ML_DIM, ML_DIM), lambda i: (i, l, 0, 0, 0, 0)),
            pl.BlockSpec((nb, 2, ML_HEADS, ML_DIM), lambda i: (i, 0, 0, 0)),
            pl.BlockSpec((nb, 2, ML_HEADS, 128), lambda i: (i, 0, 0, 0)),
        ]
    n_chain = nb * ML_HEADS * 2
    return pl.pallas_call(
        functools.partial(_ml_kernel, t_len=t_len, nb=nb, unroll=2,
                          zero_init=prompt, emit_state=prompt),
        grid=(n_seq // nb,),
        in_specs=in_specs,
        out_specs=out_specs,
        out_shape=out_shape,
        input_output_aliases=aliases,
        scratch_shapes=[pltpu.VMEM((rows, 512), F32), pltpu.VMEM((rows, 128), F32), pltpu.VMEM((rows, 128), F32),
                        pltpu.VMEM((rows // ML_LC, 16, ML_LC), F32),
                        pltpu.VMEM((n_chain * ML_DIM, 2 * ML_DIM), F32), pltpu.VMEM((n_chain, 128), F32)],
        compiler_params=_cparams(("arbitrary",), 56),
        name="mlstm_ctx" if prompt else "mlstm_lat",
    )(*args)


def _merge_kernel(x_ref, mod_ref, oda_ref, oml_ref, ona_ref, g0_ref, g1_ref, g2_ref,
                  wda_ref, wml_ref, wna_ref, wo_ref, o_ref):
    def branch(o, w, g):
        return jax.nn.sigmoid(g[...].astype(F32)) * _dot(o[...], w[...])

    merged = (branch(oda_ref, wda_ref, g0_ref) + branch(oml_ref, wml_ref, g1_ref)
              + branch(ona_ref, wna_ref, g2_ref))
    o_ref[...] = x_ref[...] + mod_ref[2:3, :] * _dot(merged.astype(BF16), wo_ref[...])


def _merge_out(x, modl, o_da, o_ml, o_na, cols, w_da, w_ml, w_na, w_out, *, prompt):
    m = x.shape[0]
    tm = 512
    const = lambda i: (0, 0)
    a, gt = cols["merge"]
    g0 = gt // 2
    return pl.pallas_call(
        _merge_kernel,
        grid=(m // tm,),
        in_specs=[
            pl.BlockSpec((tm, D_MODEL), lambda i: (i, 0)),
            pl.BlockSpec((None, 6, D_MODEL), _mod_row_map(prompt, tm)),
            pl.BlockSpec((tm, 512), lambda i: (i, 0)),
            pl.BlockSpec((tm, 512), lambda i: (i, 0)),
            pl.BlockSpec((tm, 512), lambda i: (i, 0)),
            pl.BlockSpec((tm, D_MODEL), lambda i: (i, g0)),
            pl.BlockSpec((tm, D_MODEL), lambda i: (i, g0 + 1)),
            pl.BlockSpec((tm, D_MODEL), lambda i: (i, g0 + 2)),
            pl.BlockSpec((512, D_MODEL), const),
            pl.BlockSpec((512, D_MODEL), const),
            pl.BlockSpec((512, D_MODEL), const),
            pl.BlockSpec((D_MODEL, D_MODEL), const),
        ],
        out_specs=pl.BlockSpec((tm, D_MODEL), lambda i: (i, 0)),
        out_shape=jax.ShapeDtypeStruct((m, D_MODEL), F32),
        compiler_params=_cparams(("arbitrary",), 48),
        name="merge_out_ctx" if prompt else "merge_out_lat",
    )(x, modl, o_da, o_ml, o_na, a, a, a, w_da, w_ml, w_na, w_out)


def _ffn_kernel(x_ref, mod_ref, g_ref, w1_ref, b1_ref, w2_ref, b2_ref, *refs, final):
    o_ref, h_scr, acc_scr = refs[-3:]
    f = pl.program_id(1)

    @pl.when(f == 0)
    def _():
        h_scr[...] = _modulated_norm(x_ref[...], g_ref[...], mod_ref[3:4, :], mod_ref[4:5, :]).astype(BF16)
        acc_scr[...] = jnp.zeros_like(acc_scr)

    h = h_scr[...]
    part = None
    for c in range(w1_ref.shape[1] // IP_SUB):
        cs = slice(c * IP_SUB, (c + 1) * IP_SUB)
        u = jnp.maximum(_dot(h, w1_ref[:, cs]) + b1_ref[:, cs], 0.0)
        d = _dot((u * u).astype(BF16), w2_ref[cs, :])
        part = d if part is None else part + d
    acc_scr[...] += part

    @pl.when(f == pl.num_programs(1) - 1)
    def _():
        y = x_ref[...] + mod_ref[5:6, :] * (acc_scr[...] + b2_ref[...])
        if final:
            ms = jnp.mean(y * y, axis=-1, keepdims=True)
            y = y * lax.rsqrt(ms + EPS) * refs[0][...]
        o_ref[...] = y


def _ffn(x, modl, g2, w1, b1, w2, b2, g_final, *, prompt):
    m = x.shape[0]
    tm, tf = 1024, 1024
    final = g_final is not None
    in_specs = [
        pl.BlockSpec((tm, D_MODEL), lambda i, f: (i, 0)),
        pl.BlockSpec((None, 6, D_MODEL), _mod_row_map(prompt, tm)),
        pl.BlockSpec((1, D_MODEL), lambda i, f: (0, 0)),
        pl.BlockSpec((D_MODEL, tf), lambda i, f: (0, f)),
        pl.BlockSpec((1, tf), lambda i, f: (0, f)),
        pl.BlockSpec((tf, D_MODEL), lambda i, f: (f, 0)),
        pl.BlockSpec((1, D_MODEL), lambda i, f: (0, 0)),
    ]
    args = [x, modl, g2, w1, b1, w2, b2]
    if final:
        in_specs.append(pl.BlockSpec((1, D_MODEL), lambda i, f: (0, 0)))
        args.append(g_final)
    return pl.pallas_call(
        functools.partial(_ffn_kernel, final=final),
        grid=(m // tm, D_FF // tf),
        in_specs=in_specs,
        out_specs=pl.BlockSpec((tm, D_MODEL), lambda i, f: (i, 0)),
        out_shape=jax.ShapeDtypeStruct((m, D_MODEL), F32),
        scratch_shapes=[pltpu.VMEM((tm, D_MODEL), BF16), pltpu.VMEM((tm, D_MODEL), F32)],
        compiler_params=_cparams(("arbitrary", "arbitrary"), 48),
        name="ffn_ctx" if prompt else "ffn_lat",
    )(*args)


def _rope_tables():
    t = jnp.arange(DEC_SEQ)
    lane = jnp.arange(128)
    sub = lane % 32
    freq = ROPE_BASE ** (-(2 * (sub % 16)).astype(F32) / 32.0)
    use_row = (lane % 64) < 32
    posv = jnp.where(use_row[None, :], (t // GRID_W)[:, None], (t % GRID_W)[:, None]).astype(F32)
    ang = posv * freq[None, :]
    sign = jnp.where(sub < 16, -1.0, 1.0).astype(F32)
    return jnp.cos(ang), jnp.sin(ang) * sign[None, :]


def kernel(x_prompt, x_sample, cache_da_k, cache_da_v, cache_na_k, cache_na_v, state_ml_C, state_ml_n,
           state_ml_m, c, c_ctx, w_mod, b_mod, norm1, w_in, b_in, da_lam, da_subln, ml_norm, na_rpb,
           w_up_da, w_up_ml, w_up_na, w_out, norm2, w_ff1, b_ff1, w_ff2, b_ff2, norm_f):
    xp = x_prompt.reshape(BATCH * SEQ, D_MODEL)
    xs = x_sample.reshape(DEC_BATCH * DEC_SEQ, D_MODEL)

    segs = ((512, 1024), (1024, 1536), (4112, 4624), (4624, 5136), (0, 512), (1536, GATE_OFF),
            (GATE_OFF + 16, 4112), (5136, 8208))
    w_main = jnp.concatenate([w_in[..., a:b] for a, b in segs], axis=-1).astype(BF16)
    b_main = jnp.concatenate([b_in[..., a:b] for a, b in segs], axis=-1).reshape(DEPTH, 1, N_MAIN)
    tiles = ("da_k", "da_v", "na_k", "na_v", "da_q", "ml_q", "ml_k", "ml_v", "ml_o", "na_q", "merge")
    tile_of = {n: t for t, n in enumerate(tiles)}
    src, dst = [], []
    for hh in range(ML_HEADS):
        for dr in range(2):
            for tt in range(2):
                src.append(GATE_OFF + dr * 2 * ML_HEADS + tt * ML_HEADS + hh)
                dst.append(hh * 4 + dr * 2 + tt)
    src, dst = np.array(src), np.array(dst)
    w_g = jnp.zeros((DEPTH, D_MODEL, 128), F32).at[:, :, dst].set(w_in[:, :, src]).astype(BF16)
    b_g = jnp.zeros((DEPTH, 1, 128), F32).at[:, 0, dst].set(b_in[:, src])
    cs = np.ones((1, N_MAIN), np.float32)
    for name, scale in (("da_q", DA_QK ** -0.5), ("ml_k", ML_DIM ** -0.5), ("na_q", NA_DIM ** -0.5)):
        cs[0, tile_of[name] * 512:(tile_of[name] + 1) * 512] = scale
    cscale = jnp.asarray(cs)
    w_da_b, w_ml_b, w_na_b, w_out_b = (w.astype(BF16) for w in (w_up_da, w_up_ml, w_up_na, w_out))
    w_ff1_b, w_ff2_b = w_ff1.astype(BF16), w_ff2.astype(BF16)
    rope_tabs = _rope_tables()

    cc = jnp.zeros((8, D_MODEL), F32).at[0].set(c_ctx).at[1:1 + DEC_BATCH].set(c)
    mod = _modulation(cc, w_mod, b_mod).reshape(DEPTH, 8, 6, D_MODEL)

    cdk = cache_da_k.reshape(DEC_BATCH, DEPTH, PAST_LEN, 512)
    cdv = cache_da_v.reshape(DEC_BATCH, DEPTH, PAST_LEN, 512)
    cnk = cache_na_k.reshape(DEC_BATCH, DEPTH, PAST_LEN, 512)
    cnv = cache_na_v.reshape(DEC_BATCH, DEPTH, PAST_LEN, 512)

    kv_outs = [jnp.zeros((BATCH, DEPTH, SEQ, 512), F32) for _ in range(4)]
    c_out = jnp.zeros((BATCH, DEPTH, 2, ML_HEADS, ML_DIM, ML_DIM), F32)
    coll_n, coll_m = [], []
    gf = norm_f.reshape(1, D_MODEL)
    for l in range(DEPTH):
        lam_init = 0.8 - 0.6 * math.exp(-0.3 * l)
        modl = mod[l]
        g1 = norm1[l].reshape(1, D_MODEL)
        g2 = norm2[l].reshape(1, D_MODEL)
        sub_row = da_subln[l].reshape(1, 128)
        bias_tab = _na_bias_table(na_rpb[l])
        for prompt in (True, False):
            x = xp if prompt else xs
            if prompt:
                a_kv, kv_outs = _in_proj_kv(x, modl, g1, w_main[l], b_main[l], kv_outs, l)
                a, gates = _in_proj(x, modl, g1, w_main[l], b_main[l], cscale, w_g[l], b_g[l],
                                    col0=N_KV, prompt=True)
                cols = {n: (a_kv, t) if t < 4 else (a, t - 4) for n, t in tile_of.items()}
            else:
                a, gates = _in_proj(x, modl, g1, w_main[l], b_main[l], cscale, w_g[l], b_g[l],
                                    col0=0, prompt=False)
                cols = {n: (a, t) for n, t in tile_of.items()}
            o_da = _diff_attention(cols, cdk, cdv, rope_tabs, l, da_lam[l], sub_row, lam_init, prompt=prompt)
            ml_out = _mlstm(cols, gates, (state_ml_C, state_ml_n, state_ml_m), c_out, l, ml_norm[l],
                            prompt=prompt)
            o_ml = ml_out[0]
            if prompt:
                c_out = ml_out[1]
                coll_n.append(ml_out[2])
                coll_m.append(ml_out[3][..., 0])
                o_na = _na_ctx(cols)
            else:
                o_na = _na_lat(cols, cnk, cnv, l, bias_tab)
            x = _merge_out(x, modl, o_da, o_ml, o_na, cols, w_da_b[l], w_ml_b[l], w_na_b[l], w_out_b[l],
                           prompt=prompt)
            x = _ffn(x, modl, g2, w_ff1_b[l], b_ff1[l].reshape(1, D_FF), w_ff2_b[l],
                     b_ff2[l].reshape(1, D_MODEL), gf if l == DEPTH - 1 else None, prompt=prompt)
            if prompt:
                xp = x
            else:
                xs = x

    y_prompt = xp.reshape(BATCH, SEQ, D_MODEL)
    y_sample = xs.reshape(DEC_BATCH, DEC_SEQ, D_MODEL)
    new_da_k = kv_outs[0].reshape(BATCH, DEPTH, SEQ, DA_HEADS, 2 * DA_QK)
    new_da_v = kv_outs[1].reshape(BATCH, DEPTH, SEQ, DA_HEADS, 2 * DA_QK)
    new_na_k = kv_outs[2].reshape(BATCH, DEPTH, SEQ, NA_HEADS, NA_DIM)
    new_na_v = kv_outs[3].reshape(BATCH, DEPTH, SEQ, NA_HEADS, NA_DIM)
    new_ml_n = jnp.stack(coll_n, axis=1)
    new_ml_m = jnp.stack(coll_m, axis=1)
    return (y_prompt, y_sample, new_da_k, new_da_v, new_na_k, new_na_v, c_out, new_ml_n, new_ml_m)
```

```python
import functools
import math

import jax
import jax.numpy as jnp
import numpy as np
from jax import lax
from jax.experimental import pallas as pl
from jax.experimental.pallas import tpu as pltpu

F32 = jnp.float32
BF16 = jnp.bfloat16
I32 = jnp.int32

D_MODEL = 1024
BATCH = 32
SEQ = 256
DEPTH = 4
DEC_BATCH = 2
DEC_SEQ = 4096
PAST_LEN = 512
GRID_W = 64
DA_HEADS = 4
DA_QK = 64
ML_HEADS = 4
ML_DIM = 128
ML_LC = 256
NA_HEADS = 8
NA_DIM = 64
NA_WIN_ROWS = 8
NA_WIN_COLS = 16
D_FF = 4 * D_MODEL
ROPE_BASE = 10000.0
EPS = 1e-6
N_MAIN = 8192
GATE_OFF = 3584
NEG = -1e30

MIB = 1024 * 1024
NT_DIMS = (((1,), (1,)), ((), ()))


def _cparams(sem, vmem_mib):
    return pltpu.CompilerParams(dimension_semantics=sem, vmem_limit_bytes=vmem_mib * MIB)


def _dot(a, b):
    return jnp.dot(a, b, preferred_element_type=F32)


def _dot_nt(a, b):
    return lax.dot_general(a, b, NT_DIMS, preferred_element_type=F32)


def _mod_kernel(c_ref, w_ref, b_ref, o_ref):
    c = c_ref[...]
    s = (c * jax.nn.sigmoid(c)).astype(BF16)
    o_ref[0] = _dot(s, w_ref[0].astype(BF16)) + b_ref[0]


def _modulation(cc, w_mod, b_mod):
    tn = 1536
    n = 6 * D_MODEL
    return pl.pallas_call(
        _mod_kernel,
        grid=(DEPTH, n // tn),
        in_specs=[
            pl.BlockSpec((8, D_MODEL), lambda l, j: (0, 0)),
            pl.BlockSpec((1, D_MODEL, tn), lambda l, j: (l, 0, j)),
            pl.BlockSpec((1, 1, tn), lambda l, j: (l, 0, j)),
        ],
        out_specs=pl.BlockSpec((1, 8, tn), lambda l, j: (l, 0, j)),
        out_shape=jax.ShapeDtypeStruct((DEPTH, 8, n), F32),
        compiler_params=_cparams(("arbitrary", "arbitrary"), 40),
        name="modulation",
    )(cc, w_mod, b_mod.reshape(DEPTH, 1, n))


def _modulated_norm(x, g, shift, scale):
    ms = jnp.mean(x * x, axis=-1, keepdims=True)
    y = x * lax.rsqrt(ms + EPS) * g
    return y * (1.0 + scale) + shift


IP_TM = 1024
IP_SUB = 512
N_KV = 2048


def _inproj_kernel(x_ref, mod_ref, g_ref, w_ref, b_ref, cs_ref, wg_ref, bg_ref, a_ref, gate_ref, h_scr):
    @pl.when(pl.program_id(1) == 0)
    def _():
        h = _modulated_norm(x_ref[...], g_ref[...], mod_ref[0:1, :], mod_ref[1:2, :]).astype(BF16)
        h_scr[...] = h
        gate_ref[...] = _dot(h, wg_ref[...]) + bg_ref[...]

    h = h_scr[...]
    for c in range(a_ref.shape[1] // IP_SUB):
        cs = slice(c * IP_SUB, (c + 1) * IP_SUB)
        a_ref[:, cs] = ((_dot(h, w_ref[:, cs]) + b_ref[:, cs]) * cs_ref[:, cs]).astype(BF16)


def _inproj_kv_kernel(x_ref, mod_ref, g_ref, w_ref, b_ref, *refs):
    a_ref = refs[4]
    f32_refs = refs[5:9]
    h = _modulated_norm(x_ref[...], g_ref[...], mod_ref[0:1, :], mod_ref[1:2, :]).astype(BF16)
    for c, ref in enumerate(f32_refs):
        cs = slice(c * IP_SUB, (c + 1) * IP_SUB)
        acc = _dot(h, w_ref[:, cs]) + b_ref[:, cs]
        if c < 2:
            ref[...] = acc.reshape(ref.shape)
        else:
            for bb in range(ref.shape[0]):
                ref[bb] = acc[bb * SEQ:(bb + 1) * SEQ, :].T.reshape(ref.shape[1:])
        a_ref[:, cs] = acc.astype(BF16)


def _mod_row_map(prompt, tm):
    if prompt:
        return lambda i, *_: (0, 0, 0)
    return lambda i, *_: (1 + i // (DEC_SEQ // tm), 0, 0)


def _in_proj(x, modl, g1, w_main, b_main, cscale, w_g, b_g, *, col0, prompt):
    m = x.shape[0]
    tm, tn = IP_TM, 1024
    j0 = col0 // tn
    return pl.pallas_call(
        _inproj_kernel,
        grid=(m // tm, (N_MAIN - col0) // tn),
        in_specs=[
            pl.BlockSpec((tm, D_MODEL), lambda i, j: (i, 0)),
            pl.BlockSpec((None, 6, D_MODEL), _mod_row_map(prompt, tm)),
            pl.BlockSpec((1, D_MODEL), lambda i, j: (0, 0)),
            pl.BlockSpec((D_MODEL, tn), lambda i, j: (0, j + j0)),
            pl.BlockSpec((1, tn), lambda i, j: (0, j + j0)),
            pl.BlockSpec((1, tn), lambda i, j: (0, j + j0)),
            pl.BlockSpec((D_MODEL, 128), lambda i, j: (0, 0)),
            pl.BlockSpec((1, 128), lambda i, j: (0, 0)),
        ],
        out_specs=[pl.BlockSpec((tm, tn), lambda i, j: (i, j)),
                   pl.BlockSpec((tm, 128), lambda i, j: (i, 0))],
        out_shape=[jax.ShapeDtypeStruct((m, N_MAIN - col0), BF16), jax.ShapeDtypeStruct((m, 128), F32)],
        scratch_shapes=[pltpu.VMEM((tm, D_MODEL), BF16)],
        compiler_params=_cparams(("arbitrary", "arbitrary"), 48),
        name="in_proj_ctx" if prompt else "in_proj_lat",
    )(x, modl, g1, w_main, b_main, cscale, w_g, b_g)


def _in_proj_kv(x, modl, g1, w_main, b_main, kv_outs, l):
    m = x.shape[0]
    tm = IP_TM
    nbat = tm // SEQ
    kv_specs = [pl.BlockSpec((nbat, None) + o.shape[2:], functools.partial(lambda i, nd: (i, l) + (0,) * nd,
                                                                            nd=o.ndim - 2)) for o in kv_outs]
    outs = pl.pallas_call(
        _inproj_kv_kernel,
        grid=(m // tm,),
        in_specs=[
            pl.BlockSpec((tm, D_MODEL), lambda i: (i, 0)),
            pl.BlockSpec((None, 6, D_MODEL), _mod_row_map(True, tm)),
            pl.BlockSpec((1, D_MODEL), lambda i: (0, 0)),
            pl.BlockSpec((D_MODEL, N_KV), lambda i: (0, 0)),
            pl.BlockSpec((1, N_KV), lambda i: (0, 0)),
        ] + [pl.BlockSpec(memory_space=pl.ANY)] * 4,
        out_specs=[pl.BlockSpec((tm, N_KV), lambda i: (i, 0))] + kv_specs,
        out_shape=[jax.ShapeDtypeStruct((m, N_KV), BF16)]
                  + [jax.ShapeDtypeStruct(o.shape, o.dtype) for o in kv_outs],
        input_output_aliases={5 + c: 1 + c for c in range(4)},
        compiler_params=_cparams(("arbitrary",), 56),
        name="in_proj_kv_ctx",
    )(x, modl, g1, w_main, b_main, *kv_outs)
    return outs[0], outs[1:]


def _half_masked(q, upper):
    lane = lax.broadcasted_iota(I32, q.shape, 1)
    keep = (lane >= 64) if upper else (lane < 64)
    return jnp.where(keep, q, jnp.zeros_like(q))


def _da_block(qm, k_t, va, m_old, acc_old):
    return _sm_block(_dot(qm, k_t), va, m_old, acc_old)


def _sm_block(s, va, m_old, acc_old):
    nl = s.shape[1] // 128
    mx = s[:, 0:128]
    for c in range(1, nl):
        mx = jnp.maximum(mx, s[:, c * 128:(c + 1) * 128])
    m_new = jnp.broadcast_to(mx.max(axis=1, keepdims=True), mx.shape)
    if m_old is not None:
        m_new = jnp.maximum(m_old, m_new)
    e = jnp.concatenate([jnp.exp(s[:, c * 128:(c + 1) * 128] - m_new) for c in range(nl)], axis=1)
    pv = _dot(e.astype(BF16), va)
    if m_old is None:
        return m_new, pv
    alpha = jnp.exp(m_old - m_new)
    return m_new, acc_old * jnp.concatenate([alpha, alpha], axis=1) + pv


def _rope(x, cos, sin):
    lane = lax.broadcasted_iota(I32, x.shape, 1)
    partner = jnp.where((lane & 16) == 0, pltpu.roll(x, 112, 1), pltpu.roll(x, 16, 1))
    return x * cos + partner * sin


def _da_kernel(*refs, hp, s_new, s_cache, kb, rope, lam_init):
    q_ref, k_ref, v_ref = refs[:3]
    pos = 3
    if s_cache:
        ck_ref, cv_ref = refs[pos:pos + 2]
        pos += 2
    if rope:
        cos_ref, sin_ref = refs[pos:pos + 2]
        pos += 2
    lam_ref, sub_ref, o_ref, kt_scr, va_scr, m_scr, acc_scr = refs[pos:pos + 7]
    qi = pl.program_id(2)
    tq = q_ref.shape[0]
    n_new, n_all = s_new // kb, (s_new + s_cache) // kb

    @pl.when(qi == 0)
    def _():
        for hh in range(hp):
            sl = slice(hh * 128, (hh + 1) * 128)
            for j in range(n_new):
                rows = slice(j * kb, (j + 1) * kb)
                kblk = k_ref[rows, sl].astype(F32)
                if rope:
                    kblk = _rope(kblk, cos_ref[rows, :], sin_ref[rows, :])
                kt_scr[hh, j] = kblk.T.astype(BF16)
            va_scr[hh, 0:s_new, 0:128] = v_ref[:, sl]
            for j in range(n_all - n_new):
                kt_scr[hh, n_new + j] = ck_ref[j * kb:(j + 1) * kb, sl].T.astype(BF16)
            if s_cache:
                va_scr[hh, s_new:s_new + s_cache, 0:128] = cv_ref[:, sl].astype(BF16)
            va_scr[hh, :, 128:256] = jnp.ones((s_new + s_cache, 128), BF16)

    lv = lam_ref[...]
    lam = (jnp.exp(jnp.sum(lv[0:1] * lv[1:2], axis=1, keepdims=True))
           - jnp.exp(jnp.sum(lv[2:3] * lv[3:4], axis=1, keepdims=True)) + lam_init)
    for hh in range(hp):
        sl = slice(hh * 128, (hh + 1) * 128)
        q = q_ref[:, sl]
        if rope:
            q_rows = pl.ds(pl.multiple_of(qi * tq, tq), tq)
            q = _rope(q.astype(F32), cos_ref[q_rows, :], sin_ref[q_rows, :]).astype(BF16)
        qms = (_half_masked(q, False), _half_masked(q, True))
        if n_all == 1:
            accs = [_da_block(qm, kt_scr[hh, 0], va_scr[hh], None, None)[1] for qm in qms]
        else:
            m_scr[...] = jnp.full(m_scr.shape, NEG, F32)
            acc_scr[...] = jnp.zeros_like(acc_scr)

            def body(j, carry, hh=hh, qms=qms):
                va = va_scr[hh, pl.ds(pl.multiple_of(j * kb, kb), kb), :]
                k_t = kt_scr[hh, j]
                new = [_da_block(qm, k_t, va, m_scr[mp], acc_scr[mp]) for mp, qm in enumerate(qms)]
                for mp, (m_new, acc) in enumerate(new):
                    m_scr[mp] = m_new
                    acc_scr[mp] = acc
                return carry

            lax.fori_loop(0, n_all, body, 0, unroll=True)
            accs = [acc_scr[0], acc_scr[1]]
        o1 = accs[0][:, :128] / accs[0][:, 128:]
        o2 = accs[1][:, :128] / accs[1][:, 128:]
        o = o1 - lam * o2
        ms = jnp.mean(o * o, axis=1, keepdims=True)
        on = o * lax.rsqrt(ms + EPS) * sub_ref[...] * (1.0 - lam_init)
        o_ref[:, sl] = on.astype(BF16)


def _diff_attention(cols, cache_k, cache_v, rope_tabs, l, da_lam_l, subln_row, lam_init, *, prompt):
    (qa, qt), (ka, kt), (va, vt) = cols["da_q"], cols["da_k"], cols["da_v"]
    m = qa.shape[0]
    if prompt:
        hp, tq, s_new, s_cache, kb = DA_HEADS, SEQ, SEQ, 0, SEQ
        grid = (BATCH, 1, 1)
        in_specs = [
            pl.BlockSpec((SEQ, 512), lambda b, h, qi: (b, qt)),
            pl.BlockSpec((SEQ, 512), lambda b, h, qi: (b, kt)),
            pl.BlockSpec((SEQ, 512), lambda b, h, qi: (b, vt)),
        ]
        args = [qa, ka, va]
        out_spec = pl.BlockSpec((SEQ, 512), lambda b, h, qi: (b, 0))
    else:
        hp, tq, s_new, s_cache, kb = 1, 1024, DEC_SEQ, PAST_LEN, 512
        nq = DEC_SEQ // tq
        grid = (DEC_BATCH, DA_HEADS, nq)
        in_specs = [
            pl.BlockSpec((tq, 128), lambda b, h, qi: (b * nq + qi, 4 * qt + h)),
            pl.BlockSpec((DEC_SEQ, 128), lambda b, h, qi: (b, 4 * kt + h)),
            pl.BlockSpec((DEC_SEQ, 128), lambda b, h, qi: (b, 4 * vt + h)),
            pl.BlockSpec((None, None, PAST_LEN, 128), lambda b, h, qi: (b, l, 0, h)),
            pl.BlockSpec((None, None, PAST_LEN, 128), lambda b, h, qi: (b, l, 0, h)),
            pl.BlockSpec((DEC_SEQ, 128), lambda b, h, qi: (0, 0)),
            pl.BlockSpec((DEC_SEQ, 128), lambda b, h, qi: (0, 0)),
        ]
        args = [qa, ka, va, cache_k, cache_v, *rope_tabs]
        out_spec = pl.BlockSpec((tq, 128), lambda b, h, qi: (b * nq + qi, h))
    in_specs += [
        pl.BlockSpec((4, DA_QK), lambda b, h, qi: (0, 0)),
        pl.BlockSpec((1, 128), lambda b, h, qi: (0, 0)),
    ]
    args += [da_lam_l, subln_row]
    s_all = s_new + s_cache
    return pl.pallas_call(
        functools.partial(_da_kernel, hp=hp, s_new=s_new, s_cache=s_cache, kb=kb, rope=not prompt,
                          lam_init=lam_init),
        grid=grid,
        in_specs=in_specs,
        out_specs=out_spec,
        out_shape=jax.ShapeDtypeStruct((m, 512), BF16),
        scratch_shapes=[pltpu.VMEM((hp, s_all // kb, 128, kb), BF16), pltpu.VMEM((hp, s_all, 256), BF16),
                        pltpu.VMEM((2, tq, 128), F32), pltpu.VMEM((2, tq, 256), F32)],
        compiler_params=_cparams(("arbitrary", "arbitrary", "arbitrary"), 48),
        name="diff_attn_ctx" if prompt else "diff_attn_lat",
    )(*args)


def _na_bias_kernel(rpb_ref, o_ref):
    h = pl.program_id(0)
    n_dc = 2 * NA_WIN_COLS - 1
    n_dr = 2 * NA_WIN_ROWS - 1
    qc = lax.broadcasted_iota(I32, (GRID_W, 128), 0)
    lane = lax.broadcasted_iota(I32, (GRID_W, 128), 1)
    kc = lane & (GRID_W - 1)
    dcm = kc - qc + (NA_WIN_COLS - 1)
    cstart = jnp.clip(qc - NA_WIN_COLS // 2, 0, GRID_W - NA_WIN_COLS)
    left = lane < GRID_W
    base = h * (n_dr * n_dc)
    for e in range(16):
        acc = jnp.zeros((GRID_W, 128), F32)
        for dc in range(n_dc):
            lv = rpb_ref[base + (e - 1) * n_dc + dc] if e >= 1 else 0.0
            rv = rpb_ref[base + e * n_dc + dc] if e < n_dr else 0.0
            acc = jnp.where(dcm == dc, jnp.where(left, lv, rv), acc)
        o_ref[0, e] = jnp.where(kc >= cstart, jnp.where(kc < cstart + NA_WIN_COLS, acc, NEG), NEG)


def _na_bias_table(rpb_l):
    return pl.pallas_call(
        _na_bias_kernel,
        grid=(NA_HEADS,),
        in_specs=[pl.BlockSpec(memory_space=pltpu.SMEM)],
        out_specs=pl.BlockSpec((1, 16, GRID_W, 128), lambda h: (h, 0, 0, 0)),
        out_shape=jax.ShapeDtypeStruct((NA_HEADS, 16, GRID_W, 128), F32),
        compiler_params=_cparams(("arbitrary",), 16),
        name="na_bias_table",
    )(rpb_l.reshape(-1))


def _na_lat_kernel(q_ref, k_ref, v_ref, ck_ref, cv_ref, bp_ref, o_ref):
    rb = pl.program_id(1)
    ws = jnp.clip(2 * rb - 1, 0, 12)
    delta = 4 * ws - 8 * rb
    tok0 = pl.multiple_of(ws * 256, 256)
    qr = 8 * rb + (lax.broadcasted_iota(I32, (512, 1024), 0) >> 6)
    kr = 4 * ws + (lax.broadcasted_iota(I32, (512, 1024), 1) >> 6)
    st = jnp.clip(qr - NA_WIN_ROWS // 2, 0, GRID_W - NA_WIN_ROWS)
    rowmask = jnp.where(kr >= st, jnp.where(kr < st + NA_WIN_ROWS, 0.0, NEG), NEG)
    ones_b = jnp.ones((1024 + PAST_LEN, 128), BF16)
    lane = lax.broadcasted_iota(I32, (512, 128), 1)
    for g in range(NA_HEADS // 2):
        sl = slice(g * 128, (g + 1) * 128)
        q2 = q_ref[:, sl]
        kall = jnp.concatenate([k_ref[pl.ds(tok0, 1024), sl].astype(F32), ck_ref[:, sl]], axis=0)
        k_t = kall.T.astype(BF16)
        vall = jnp.concatenate([v_ref[pl.ds(tok0, 1024), sl], cv_ref[:, sl].astype(BF16)], axis=0)
        va = jnp.concatenate([vall, ones_b], axis=1)
        outs = []
        for par in range(2):
            h = 2 * g + par
            s = _dot(_half_masked(q2, par == 1), k_t)
            rows = []
            for qrl in range(8):
                tiles = [bp_ref[h, jnp.clip(delta + 2 * jk - qrl + 8, 0, 15)] for jk in range(8)]
                rows.append(jnp.concatenate(tiles, axis=1))
            bias = jnp.concatenate(rows, axis=0)
            s = jnp.concatenate([s[:, :1024] + bias + rowmask, s[:, 1024:]], axis=1)
            acc = _sm_block(s, va, None, None)[1]
            outs.append(acc[:, :128] / acc[:, 128:])
        o_ref[:, sl] = jnp.where(lane < 64, outs[0], outs[1]).astype(BF16)


def _na_lat(cols, cache_k, cache_v, l, bias_tab):
    (qa, qt), (ka, kt), (va, vt) = cols["na_q"], cols["na_k"], cols["na_v"]
    return pl.pallas_call(
        _na_lat_kernel,
        grid=(DEC_BATCH, 8),
        in_specs=[
            pl.BlockSpec((512, 512), lambda b, rb: (b * 8 + rb, qt)),
            pl.BlockSpec((DEC_SEQ, 512), lambda b, rb: (b, kt)),
            pl.BlockSpec((DEC_SEQ, 512), lambda b, rb: (b, vt)),
            pl.BlockSpec((None, None, PAST_LEN, 512), lambda b, rb: (b, l, 0, 0)),
            pl.BlockSpec((None, None, PAST_LEN, 512), lambda b, rb: (b, l, 0, 0)),
            pl.BlockSpec((NA_HEADS, 16, GRID_W, 128), lambda b, rb: (0, 0, 0, 0)),
        ],
        out_specs=pl.BlockSpec((512, 512), lambda b, rb: (b * 8 + rb, 0)),
        out_shape=jax.ShapeDtypeStruct((DEC_BATCH * DEC_SEQ, 512), BF16),
        compiler_params=_cparams(("arbitrary", "arbitrary"), 56),
        name="nbr_attn_lat",
    )(qa, ka, va, cache_k, cache_v, bias_tab)


def _na_ctx_kernel(q_ref, k_ref, v_ref, o_ref):
    ones_b = jnp.ones((q_ref.shape[0], 128), BF16)
    lane = lax.broadcasted_iota(I32, (q_ref.shape[0], 128), 1)
    for g in range(NA_HEADS // 2):
        sl = slice(g * 128, (g + 1) * 128)
        q2 = q_ref[:, sl]
        k_t = k_ref[:, sl].astype(F32).T.astype(BF16)
        va = jnp.concatenate([v_ref[:, sl], ones_b], axis=1)
        acc_a = _da_block(_half_masked(q2, False), k_t, va, None, None)[1]
        acc_b = _da_block(_half_masked(q2, True), k_t, va, None, None)[1]
        o_pair = jnp.where(lane < 64, acc_a[:, :128] / acc_a[:, 128:], acc_b[:, :128] / acc_b[:, 128:])
        o_ref[:, sl] = o_pair.astype(BF16)


def _na_ctx(cols):
    (qa, qt), (ka, kt), (va, vt) = cols["na_q"], cols["na_k"], cols["na_v"]
    return pl.pallas_call(
        _na_ctx_kernel,
        grid=(BATCH,),
        in_specs=[
            pl.BlockSpec((SEQ, 512), lambda b: (b, qt)),
            pl.BlockSpec((SEQ, 512), lambda b: (b, kt)),
            pl.BlockSpec((SEQ, 512), lambda b: (b, vt)),
        ],
        out_specs=pl.BlockSpec((SEQ, 512), lambda b: (b, 0)),
        out_shape=jax.ShapeDtypeStruct((BATCH * SEQ, 512), BF16),
        compiler_params=_cparams(("arbitrary",), 32),
        name="nbr_attn_ctx",
    )(qa, ka, va)


def _log_sigmoid(x):
    return jnp.minimum(x, 0.0) - jnp.log1p(jnp.exp(-jnp.abs(x)))


def _split3(x):
    x1 = x.astype(BF16)
    r1 = x - x1.astype(F32)
    x2 = r1.astype(BF16)
    x3 = (r1 - x2.astype(F32)).astype(BF16)
    return x1, x2, x3


def _ml_kernel(*refs, t_len, nb, unroll, zero_init, emit_state):
    q_ref, k_ref, v_ref, og_ref, g_ref = refs[:5]
    pos = 5
    if not zero_init:
        c0_ref, n0_ref, m0_ref = refs[pos:pos + 3]
        pos += 3
    nrm_ref = refs[pos]
    pos += 2 if emit_state else 1
    o_ref = refs[pos]
    pos += 1
    if emit_state:
        cst_ref, nst_ref, mst_ref = refs[pos:pos + 3]
        pos += 3
    hsum_scr, colb_scr, cmb_scr, rowt_scr, s_scr, m_scr = refs[pos:pos + 6]
    step = pl.program_id(0)
    lc = ML_LC
    nc = t_len // lc
    chains = [(bi, h, d) for bi in range(nb) for h in range(ML_HEADS) for d in range(2)]

    ri = lax.broadcasted_iota(I32, (lc, lc), 0)
    ci = lax.broadcasted_iota(I32, (lc, lc), 1)
    lo = ri >= ci
    up = ri <= ci
    lo_b = jnp.where(lo, 1.0, 0.0).astype(BF16)
    up_b = jnp.where(up, 1.0, 0.0).astype(BF16)
    lane = lax.broadcasted_iota(I32, (lc, 128), 1)
    trow = lax.broadcasted_iota(I32, (lc, 128), 0)
    is_forget = (lane & 1) == 1
    is_bwd = (lane & 2) == 2

    def tri_left(tri, x):
        return sum(_dot(tri, p) for p in _split3(x))

    def prep(c, carry):
        r0 = pl.multiple_of(c * lc, lc)
        g = g_ref[pl.ds(r0, lc), :]
        lf = _log_sigmoid(g)
        cb = jnp.where(is_forget, jnp.where(is_bwd, tri_left(up_b, lf), tri_left(lo_b, lf)), g)
        colb_scr[pl.ds(r0, lc), :] = cb
        ar = pltpu.roll(cb, 127, 1) - cb
        rowt_scr[c] = ar.T[0:16]
        pre = suf = -ar
        k = 1
        while k < lc:
            pre = jnp.maximum(pre, jnp.where(trow >= k, pltpu.roll(pre, k, 0), -jnp.inf))
            suf = jnp.maximum(suf, jnp.where(trow < lc - k, pltpu.roll(suf, lc - k, 0), -jnp.inf))
            k *= 2
        cmb_scr[pl.ds(r0, lc), :] = jnp.where(is_bwd, suf, pre)
        return carry

    lax.fori_loop(0, nb * nc, prep, 0, unroll=2)

    hsum_scr[...] = jnp.zeros_like(hsum_scr)
    if zero_init:
        s_scr[...] = jnp.zeros_like(s_scr)
        m_scr[...] = jnp.zeros_like(m_scr)
    else:
        for idx, (bi, h, d) in enumerate(chains):
            n_rep = jnp.broadcast_to(n0_ref[d, h:h + 1, :], (ML_DIM, ML_DIM)).T
            s_scr[idx * ML_DIM:(idx + 1) * ML_DIM, :] = jnp.concatenate([c0_ref[d, h], n_rep], axis=1)
            m0 = m0_ref[((step * nb + bi) * 2 + d) * ML_HEADS + h]
            m_scr[idx:idx + 1, :] = jnp.full((1, 128), m0, F32)

    ones_b = jnp.ones((lc, ML_DIM), BF16)
    n_ch = len(chains)

    def body(i, carry):
        cs = [i if d == 0 else nc - 1 - i for (_, _, d) in chains]
        r0s = [pl.multiple_of(bi * t_len + c * lc, lc) for (bi, _, _), c in zip(chains, cs)]
        hsl = [slice(h * 128, (h + 1) * 128) for (_, h, _) in chains]
        qs = [q_ref[pl.ds(r0, lc), sl] for r0, sl in zip(r0s, hsl)]
        ks = [k_ref[pl.ds(r0, lc), sl] for r0, sl in zip(r0s, hsl)]
        vs = [v_ref[pl.ds(r0, lc), sl] for r0, sl in zip(r0s, hsl)]
        s_old = [s_scr[idx * ML_DIM:(idx + 1) * ML_DIM, :] for idx in range(n_ch)]
        qk = [_dot_nt(q, k) for q, k in zip(qs, ks)]

        lhs, mts, bcs, ics, mms = [], [], [], [], []
        for idx, (bi, h, d) in enumerate(chains):
            l0 = h * 4 + d * 2
            cb = colb_scr[pl.ds(r0s[idx], lc), :]
            ic = cb[:, l0:l0 + 1]
            bc = cb[:, l0 + 1:l0 + 2]
            arow = rowt_scr[bi * nc + cs[idx], l0:l0 + 1, :]
            mm = m_scr[idx:idx + 1, 0:1]
            cm = cmb_scr[pl.ds(r0s[idx], lc), :][:, l0:l0 + 1]
            gmax = jnp.maximum(mm, cm)
            mt = bc + gmax
            sc = qk[idx] * jnp.exp(jnp.where(lo if d == 0 else up, -(gmax + arow), -jnp.inf))
            inter = jnp.exp(mm - gmax)
            lhs.append(jnp.concatenate([(inter * qs[idx].astype(F32)).astype(BF16), sc.astype(BF16)], axis=1))
            mts.append(mt)
            bcs.append(bc)
            ics.append(ic)
            mms.append(mm)

        nds = [_dot(a, jnp.concatenate([so.astype(BF16), jnp.concatenate([v, ones_b], axis=1)], axis=0))
               for a, so, v in zip(lhs, s_old, vs)]

        hcs, wvs, decays, m_news = [], [], [], []
        for idx, (bi, h, d) in enumerate(chains):
            last = lc - 1 if d == 0 else 0
            nd = nds[idx]
            hcs.append(nd[:, :ML_DIM] / jnp.maximum(jnp.abs(nd[:, ML_DIM:]), jnp.exp(-mts[idx])))
            m_new = mts[idx][last:last + 1, :]
            bl = bcs[idx][last:last + 1, :]
            w = jnp.exp(bl - bcs[idx] + ics[idx] - m_new)
            decays.append(jnp.exp(bl + mms[idx] - m_new))
            m_news.append(jnp.broadcast_to(m_new, (1, 128)))
            wvs.append(jnp.concatenate([(w * vs[idx].astype(F32)).astype(BF16),
                                        jnp.broadcast_to(w, (lc, ML_DIM)).astype(BF16)], axis=1))
        kts = [k.astype(F32).T.astype(BF16) for k in ks]
        kv = [_dot(kt, wv) for kt, wv in zip(kts, wvs)]
        s_scr[...] = jnp.concatenate([dc * so + x for dc, so, x in zip(decays, s_old, kv)], axis=0)
        m_scr[...] = jnp.concatenate(m_news, axis=0)
        for bi in range(nb):
            for d in range(2):
                sel = [idx for idx, ch in enumerate(chains) if ch[0] == bi and ch[2] == d]
                hsum_scr[pl.ds(r0s[sel[0]], lc), :] += jnp.concatenate([hcs[idx] for idx in sel], axis=1)
        return carry

    lax.fori_loop(0, nc, body, 0, unroll=unroll)

    for bi in range(nb):
        rows = slice(bi * t_len, (bi + 1) * t_len)
        for h in range(ML_HEADS):
            sl = slice(h * 128, (h + 1) * 128)
            hs = hsum_scr[rows, sl]
            ms = jnp.mean(hs * hs, axis=-1, keepdims=True)
            y = hs * lax.rsqrt(ms + EPS) * nrm_ref[:, sl]
            o_ref[rows, sl] = (y * jax.nn.sigmoid(og_ref[rows, sl].astype(F32))).astype(BF16)
    if emit_state:
        for idx, (bi, h, d) in enumerate(chains):
            st = s_scr[idx * ML_DIM:(idx + 1) * ML_DIM, :]
            cst_ref[bi, d, h] = st[:, :ML_DIM]
            nst_ref[bi, d, h:h + 1, :] = st[:, ML_DIM:].T[0:1, :]
            mst_ref[bi, d, h:h + 1, :] = m_scr[idx:idx + 1, :]


def _mlstm(cols, gates, states, c_out, l, ml_norm_l, *, prompt):
    names = ("ml_q", "ml_k", "ml_v", "ml_o")
    m = cols["ml_q"][0].shape[0]
    t_len = SEQ if prompt else DEC_SEQ
    nb = 2 if prompt else 1
    n_seq = m // t_len
    rows = nb * t_len
    big = {} if prompt else dict(pipeline_mode=pl.Buffered(1))
    in_specs = [pl.BlockSpec((rows, 512), functools.partial(lambda i, t: (i, t), t=cols[n][1]), **big)
                for n in names]
    in_specs.append(pl.BlockSpec((rows, 128), lambda i: (i, 0)))
    args = [cols[n][0] for n in names] + [gates]
    if not prompt:
        c0, n0, m0 = states
        in_specs += [
            pl.BlockSpec((None, None, 2, ML_HEADS, ML_DIM, ML_DIM), lambda i: (i, l, 0, 0, 0, 0)),
            pl.BlockSpec((None, None, 2, ML_HEADS, ML_DIM), lambda i: (i, l, 0, 0, 0)),
            pl.BlockSpec(memory_space=pltpu.SMEM),
        ]
        args += [c0, n0, m0[:, l].reshape(-1)]
    in_specs.append(pl.BlockSpec((1, 512), lambda i: (0, 0)))
    args.append(ml_norm_l.reshape(1, 512))
    out_shape = [jax.ShapeDtypeStruct((m, 512), BF16)]
    out_specs = [pl.BlockSpec((rows, 512), lambda i: (i, 0))]
    aliases = {}
    if prompt:
        aliases = {len(args): 1}
        in_specs.append(pl.BlockSpec(memory_space=pl.ANY))
        args.append(c_out)
        out_shape += [
            jax.ShapeDtypeStruct(c_out.shape, F32),
            jax.ShapeDtypeStruct((n_seq, 2, ML_HEADS, ML_DIM), F32),
            jax.ShapeDtypeStruct((n_seq, 2, ML_HEADS, 128), F32),
        ]
        out_specs += [
            pl.BlockSpec((nb, None, 2, ML_HEADS, ML_DIM, ML_DIM), lambda i: (i, l, 0, 0, 0, 0)),
            pl.BlockSpec((nb, 2, ML_HEADS, ML_DIM), lambda i: (i, 0, 0, 0)),
            pl.BlockSpec((nb, 2, ML_HEADS, 128), lambda i: (i, 0, 0, 0)),
        ]
    n_chain = nb * ML_HEADS * 2
    return pl.pallas_call(
        functools.partial(_ml_kernel, t_len=t_len, nb=nb, unroll=2,
                          zero_init=prompt, emit_state=prompt),
        grid=(n_seq // nb,),
        in_specs=in_specs,
        out_specs=out_specs,
        out_shape=out_shape,
        input_output_aliases=aliases,
        scratch_shapes=[pltpu.VMEM((rows, 512), F32), pltpu.VMEM((rows, 128), F32), pltpu.VMEM((rows, 128), F32),
                        pltpu.VMEM((rows // ML_LC, 16, ML_LC), F32),
                        pltpu.VMEM((n_chain * ML_DIM, 2 * ML_DIM), F32), pltpu.VMEM((n_chain, 128), F32)],
        compiler_params=_cparams(("arbitrary",), 56),
        name="mlstm_ctx" if prompt else "mlstm_lat",
    )(*args)


def _merge_kernel(x_ref, mod_ref, oda_ref, oml_ref, ona_ref, g0_ref, g1_ref, g2_ref,
                  wda_ref, wml_ref, wna_ref, wo_ref, o_ref):
    def branch(o, w, g):
        return jax.nn.sigmoid(g[...].astype(F32)) * _dot(o[...], w[...])

    merged = (branch(oda_ref, wda_ref, g0_ref) + branch(oml_ref, wml_ref, g1_ref)
              + branch(ona_ref, wna_ref, g2_ref))
    o_ref[...] = x_ref[...] + mod_ref[2:3, :] * _dot(merged.astype(BF16), wo_ref[...])


def _merge_out(x, modl, o_da, o_ml, o_na, cols, w_da, w_ml, w_na, w_out, *, prompt):
    m = x.shape[0]
    tm = 512
    const = lambda i: (0, 0)
    a, gt = cols["merge"]
    g0 = gt // 2
    return pl.pallas_call(
        _merge_kernel,
        grid=(m // tm,),
        in_specs=[
            pl.BlockSpec((tm, D_MODEL), lambda i: (i, 0)),
            pl.BlockSpec((None, 6, D_MODEL), _mod_row_map(prompt, tm)),
            pl.BlockSpec((tm, 512), lambda i: (i, 0)),
            pl.BlockSpec((tm, 512), lambda i: (i, 0)),
            pl.BlockSpec((tm, 512), lambda i: (i, 0)),
            pl.BlockSpec((tm, D_MODEL), lambda i: (i, g0)),
            pl.BlockSpec((tm, D_MODEL), lambda i: (i, g0 + 1)),
            pl.BlockSpec((tm, D_MODEL), lambda i: (i, g0 + 2)),
            pl.BlockSpec((512, D_MODEL), const),
            pl.BlockSpec((512, D_MODEL), const),
            pl.BlockSpec((512, D_MODEL), const),
            pl.BlockSpec((D_MODEL, D_MODEL), const),
        ],
        out_specs=pl.BlockSpec((tm, D_MODEL), lambda i: (i, 0)),
        out_shape=jax.ShapeDtypeStruct((m, D_MODEL), F32),
        compiler_params=_cparams(("arbitrary",), 48),
        name="merge_out_ctx" if prompt else "merge_out_lat",
    )(x, modl, o_da, o_ml, o_na, a, a, a, w_da, w_ml, w_na, w_out)


def _ffn_kernel(x_ref, mod_ref, g_ref, w1_ref, b1_ref, w2_ref, b2_ref, *refs, final):
    o_ref, h_scr, acc_scr = refs[-3:]
    f = pl.program_id(1)

    @pl.when(f == 0)
    def _():
        h_scr[...] = _modulated_norm(x_ref[...], g_ref[...], mod_ref[3:4, :], mod_ref[4:5, :]).astype(BF16)
        acc_scr[...] = jnp.zeros_like(acc_scr)

    h = h_scr[...]
    part = None
    for c in range(w1_ref.shape[1] // IP_SUB):
        cs = slice(c * IP_SUB, (c + 1) * IP_SUB)
        u = jnp.maximum(_dot(h, w1_ref[:, cs]) + b1_ref[:, cs], 0.0)
        d = _dot((u * u).astype(BF16), w2_ref[cs, :])
        part = d if part is None else part + d
    acc_scr[...] += part

    @pl.when(f == pl.num_programs(1) - 1)
    def _():
        y = x_ref[...] + mod_ref[5:6, :] * (acc_scr[...] + b2_ref[...])
        if final:
            ms = jnp.mean(y * y, axis=-1, keepdims=True)
            y = y * lax.rsqrt(ms + EPS) * refs[0][...]
        o_ref[...] = y


def _ffn(x, modl, g2, w1, b1, w2, b2, g_final, *, prompt):
    m = x.shape[0]
    tm, tf = 1024, 1024
    final = g_final is not None
    in_specs = [
        pl.BlockSpec((tm, D_MODEL), lambda i, f: (i, 0)),
        pl.BlockSpec((None, 6, D_MODEL), _mod_row_map(prompt, tm)),
        pl.BlockSpec((1, D_MODEL), lambda i, f: (0, 0)),
        pl.BlockSpec((D_MODEL, tf), lambda i, f: (0, f)),
        pl.BlockSpec((1, tf), lambda i, f: (0, f)),
        pl.BlockSpec((tf, D_MODEL), lambda i, f: (f, 0)),
        pl.BlockSpec((1, D_MODEL), lambda i, f: (0, 0)),
    ]
    args = [x, modl, g2, w1, b1, w2, b2]
    if final:
        in_specs.append(pl.BlockSpec((1, D_MODEL), lambda i, f: (0, 0)))
        args.append(g_final)
    return pl.pallas_call(
        functools.partial(_ffn_kernel, final=final),
        grid=(m // tm, D_FF // tf),
        in_specs=in_specs,
        out_specs=pl.BlockSpec((tm, D_MODEL), lambda i, f: (i, 0)),
        out_shape=jax.ShapeDtypeStruct((m, D_MODEL), F32),
        scratch_shapes=[pltpu.VMEM((tm, D_MODEL), BF16), pltpu.VMEM((tm, D_MODEL), F32)],
        compiler_params=_cparams(("arbitrary", "arbitrary"), 48),
        name="ffn_ctx" if prompt else "ffn_lat",
    )(*args)


def _rope_tables():
    t = jnp.arange(DEC_SEQ)
    lane = jnp.arange(128)
    sub = lane % 32
    freq = ROPE_BASE ** (-(2 * (sub % 16)).astype(F32) / 32.0)
    use_row = (lane % 64) < 32
    posv = jnp.where(use_row[None, :], (t // GRID_W)[:, None], (t % GRID_W)[:, None]).astype(F32)
    ang = posv * freq[None, :]
    sign = jnp.where(sub < 16, -1.0, 1.0).astype(F32)
    return jnp.cos(ang), jnp.sin(ang) * sign[None, :]


def kernel(x_prompt, x_sample, cache_da_k, cache_da_v, cache_na_k, cache_na_v, state_ml_C, state_ml_n,
           state_ml_m, c, c_ctx, w_mod, b_mod, norm1, w_in, b_in, da_lam, da_subln, ml_norm, na_rpb,
           w_up_da, w_up_ml, w_up_na, w_out, norm2, w_ff1, b_ff1, w_ff2, b_ff2, norm_f):
    xp = x_prompt.reshape(BATCH * SEQ, D_MODEL)
    xs = x_sample.reshape(DEC_BATCH * DEC_SEQ, D_MODEL)

    segs = ((512, 1024), (1024, 1536), (4112, 4624), (4624, 5136), (0, 512), (1536, GATE_OFF),
            (GATE_OFF + 16, 4112), (5136, 8208))
    w_main = jnp.concatenate([w_in[..., a:b] for a, b in segs], axis=-1).astype(BF16)
    b_main = jnp.concatenate([b_in[..., a:b] for a, b in segs], axis=-1).reshape(DEPTH, 1, N_MAIN)
    tiles = ("da_k", "da_v", "na_k", "na_v", "da_q", "ml_q", "ml_k", "ml_v", "ml_o", "na_q", "merge")
    tile_of = {n: t for t, n in enumerate(tiles)}
    src, dst = [], []
    for hh in range(ML_HEADS):
        for dr in range(2):
            for tt in range(2):
                src.append(GATE_OFF + dr * 2 * ML_HEADS + tt * ML_HEADS + hh)
                dst.append(hh * 4 + dr * 2 + tt)
    src, dst = np.array(src), np.array(dst)
    w_g = jnp.zeros((DEPTH, D_MODEL, 128), F32).at[:, :, dst].set(w_in[:, :, src]).astype(BF16)
    b_g = jnp.zeros((DEPTH, 1, 128), F32).at[:, 0, dst].set(b_in[:, src])
    cs = np.ones((1, N_MAIN), np.float32)
    for name, scale in (("da_q", DA_QK ** -0.5), ("ml_k", ML_DIM ** -0.5), ("na_q", NA_DIM ** -0.5)):
        cs[0, tile_of[name] * 512:(tile_of[name] + 1) * 512] = scale
    cscale = jnp.asarray(cs)
    w_da_b, w_ml_b, w_na_b, w_out_b = (w.astype(BF16) for w in (w_up_da, w_up_ml, w_up_na, w_out))
    w_ff1_b, w_ff2_b = w_ff1.astype(BF16), w_ff2.astype(BF16)
    rope_tabs = _rope_tables()

    cc = jnp.zeros((8, D_MODEL), F32).at[0].set(c_ctx).at[1:1 + DEC_BATCH].set(c)
    mod = _modulation(cc, w_mod, b_mod).reshape(DEPTH, 8, 6, D_MODEL)

    cdk = cache_da_k.reshape(DEC_BATCH, DEPTH, PAST_LEN, 512)
    cdv = cache_da_v.reshape(DEC_BATCH, DEPTH, PAST_LEN, 512)
    cnk = cache_na_k.reshape(DEC_BATCH, DEPTH, PAST_LEN, 512)
    cnv = cache_na_v.reshape(DEC_BATCH, DEPTH, PAST_LEN, 512)

    kv_outs = [jnp.zeros((BATCH, DEPTH, SEQ, DA_HEADS, 2 * DA_QK), F32) for _ in range(2)]
    kv_outs += [jnp.zeros((BATCH, DEPTH, NA_HEADS, NA_DIM, SEQ), F32) for _ in range(2)]
    c_out = jnp.zeros((BATCH, DEPTH, 2, ML_HEADS, ML_DIM, ML_DIM), F32)
    coll_n, coll_m = [], []
    gf = norm_f.reshape(1, D_MODEL)
    for l in range(DEPTH):
        lam_init = 0.8 - 0.6 * math.exp(-0.3 * l)
        modl = mod[l]
        g1 = norm1[l].reshape(1, D_MODEL)
        g2 = norm2[l].reshape(1, D_MODEL)
        sub_row = da_subln[l].reshape(1, 128)
        bias_tab = _na_bias_table(na_rpb[l])
        for prompt in (True, False):
            x = xp if prompt else xs
            if prompt:
                a_kv, kv_outs = _in_proj_kv(x, modl, g1, w_main[l], b_main[l], kv_outs, l)
                a, gates = _in_proj(x, modl, g1, w_main[l], b_main[l], cscale, w_g[l], b_g[l],
                                    col0=N_KV, prompt=True)
                cols = {n: (a_kv, t) if t < 4 else (a, t - 4) for n, t in tile_of.items()}
            else:
                a, gates = _in_proj(x, modl, g1, w_main[l], b_main[l], cscale, w_g[l], b_g[l],
                                    col0=0, prompt=False)
                cols = {n: (a, t) for n, t in tile_of.items()}
            o_da = _diff_attention(cols, cdk, cdv, rope_tabs, l, da_lam[l], sub_row, lam_init, prompt=prompt)
            ml_out = _mlstm(cols, gates, (state_ml_C, state_ml_n, state_ml_m), c_out, l, ml_norm[l],
                            prompt=prompt)
            o_ml = ml_out[0]
            if prompt:
                c_out = ml_out[1]
                coll_n.append(ml_out[2])
                coll_m.append(ml_out[3][..., 0])
                o_na = _na_ctx(cols)
            else:
                o_na = _na_lat(cols, cnk, cnv, l, bias_tab)
            x = _merge_out(x, modl, o_da, o_ml, o_na, cols, w_da_b[l], w_ml_b[l], w_na_b[l], w_out_b[l],
                           prompt=prompt)
            x = _ffn(x, modl, g2, w_ff1_b[l], b_ff1[l].reshape(1, D_FF), w_ff2_b[l],
                     b_ff2[l].reshape(1, D_MODEL), gf if l == DEPTH - 1 else None, prompt=prompt)
            if prompt:
                xp = x
            else:
                xs = x

    y_prompt = xp.reshape(BATCH, SEQ, D_MODEL)
    y_sample = xs.reshape(DEC_BATCH, DEC_SEQ, D_MODEL)
    new_da_k, new_da_v = kv_outs[0], kv_outs[1]
    new_na_k = jnp.transpose(kv_outs[2], (0, 1, 4, 2, 3))
    new_na_v = jnp.transpose(kv_outs[3], (0, 1, 4, 2, 3))
    new_ml_n = jnp.stack(coll_n, axis=1)
    new_ml_m = jnp.stack(coll_m, axis=1)
    return (y_prompt, y_sample, new_da_k, new_da_v, new_na_k, new_na_v, c_out, new_ml_n, new_ml_m)
```

```python
import functools
import math

import jax
import jax.numpy as jnp
import numpy as np
from jax import lax
from jax.experimental import pallas as pl
from jax.experimental.pallas import tpu as pltpu

F32 = jnp.float32
BF16 = jnp.bfloat16
I32 = jnp.int32

D_MODEL = 1024
BATCH = 32
SEQ = 256
DEPTH = 4
DEC_BATCH = 2
DEC_SEQ = 4096
PAST_LEN = 512
GRID_W = 64
DA_HEADS = 4
DA_QK = 64
ML_HEADS = 4
ML_DIM = 128
ML_LC = 256
NA_HEADS = 8
NA_DIM = 64
NA_WIN_ROWS = 8
NA_WIN_COLS = 16
D_FF = 4 * D_MODEL
ROPE_BASE = 10000.0
EPS = 1e-6
N_MAIN = 8192
GATE_OFF = 3584
NEG = -1e30

MIB = 1024 * 1024
NT_DIMS = (((1,), (1,)), ((), ()))


def _cparams(sem, vmem_mib):
    return pltpu.CompilerParams(dimension_semantics=sem, vmem_limit_bytes=vmem_mib * MIB)


def _dot(a, b):
    return jnp.dot(a, b, preferred_element_type=F32)


def _dot_nt(a, b):
    return lax.dot_general(a, b, NT_DIMS, preferred_element_type=F32)


W_SEGS = ((512, 1024), (1024, 1536), (4112, 4624), (4624, 5136), (0, 512), (1536, GATE_OFF),
          (GATE_OFF + 16, 4112), (5136, 8208))
W_TILES = ("da_k", "da_v", "na_k", "na_v", "da_q", "ml_q", "ml_k", "ml_v", "ml_o", "na_q", "merge")


def _stage_w_kernel(w_ref, o_ref, g_ref):
    off = 0
    for a, b in W_SEGS:
        for s in range(a, b, IP_SUB):
            o_ref[0, :, off:off + IP_SUB] = w_ref[0, s:s + IP_SUB, :].T.astype(BF16)
            off += IP_SUB
    gates = w_ref[0, GATE_OFF:GATE_OFF + 16, :]
    r = lax.broadcasted_iota(I32, (128, gates.shape[1]), 0)
    g_t = jnp.zeros((128, gates.shape[1]), F32)
    for s in range(16):
        dst = (s & 3) * 4 + (s >> 3) * 2 + ((s >> 2) & 1)
        g_t = jnp.where(r == dst, gates[s:s + 1, :], g_t)
    g_ref[0] = g_t.T.astype(BF16)


def _stage_w_in(w_in_t):
    kt = 256
    n_proj = w_in_t.shape[1]
    return pl.pallas_call(
        _stage_w_kernel,
        grid=(DEPTH, D_MODEL // kt),
        in_specs=[pl.BlockSpec((1, n_proj, kt), lambda l, i: (l, 0, i))],
        out_specs=[pl.BlockSpec((1, kt, N_MAIN), lambda l, i: (l, i, 0)),
                   pl.BlockSpec((1, kt, 128), lambda l, i: (l, i, 0))],
        out_shape=[jax.ShapeDtypeStruct((DEPTH, D_MODEL, N_MAIN), BF16),
                   jax.ShapeDtypeStruct((DEPTH, D_MODEL, 128), BF16)],
        compiler_params=_cparams(("arbitrary", "arbitrary"), 48),
        name="stage_w_in",
    )(w_in_t)


def _mod_kernel(c_ref, w_ref, b_ref, o_ref):
    c = c_ref[...]
    s = (c * jax.nn.sigmoid(c)).astype(BF16)
    o_ref[0] = _dot(s, w_ref[0].astype(BF16)) + b_ref[0]


def _modulation(cc, w_mod, b_mod):
    tn = 1536
    n = 6 * D_MODEL
    return pl.pallas_call(
        _mod_kernel,
        grid=(DEPTH, n // tn),
        in_specs=[
            pl.BlockSpec((8, D_MODEL), lambda l, j: (0, 0)),
            pl.BlockSpec((1, D_MODEL, tn), lambda l, j: (l, 0, j)),
            pl.BlockSpec((1, 1, tn), lambda l, j: (l, 0, j)),
        ],
        out_specs=pl.BlockSpec((1, 8, tn), lambda l, j: (l, 0, j)),
        out_shape=jax.ShapeDtypeStruct((DEPTH, 8, n), F32),
        compiler_params=_cparams(("arbitrary", "arbitrary"), 40),
        name="modulation",
    )(cc, w_mod, b_mod.reshape(DEPTH, 1, n))


def _modulated_norm(x, g, shift, scale):
    ms = jnp.mean(x * x, axis=-1, keepdims=True)
    y = x * lax.rsqrt(ms + EPS) * g
    return y * (1.0 + scale) + shift


IP_TM = 1024
IP_SUB = 512
N_KV = 2048


def _inproj_kernel(x_ref, mod_ref, g_ref, w_ref, b_ref, cs_ref, wg_ref, bg_ref, a_ref, gate_ref, h_scr):
    @pl.when(pl.program_id(1) == 0)
    def _():
        h = _modulated_norm(x_ref[...], g_ref[...], mod_ref[0:1, :], mod_ref[1:2, :]).astype(BF16)
        h_scr[...] = h
        gate_ref[...] = _dot(h, wg_ref[...]) + bg_ref[...]

    h = h_scr[...]
    for c in range(a_ref.shape[1] // IP_SUB):
        cs = slice(c * IP_SUB, (c + 1) * IP_SUB)
        a_ref[:, cs] = ((_dot(h, w_ref[:, cs]) + b_ref[:, cs]) * cs_ref[:, cs]).astype(BF16)


def _inproj_kv_kernel(x_ref, mod_ref, g_ref, w_ref, b_ref, *refs):
    a_ref = refs[4]
    f32_refs = refs[5:9]
    h = _modulated_norm(x_ref[...], g_ref[...], mod_ref[0:1, :], mod_ref[1:2, :]).astype(BF16)
    for c, ref in enumerate(f32_refs):
        cs = slice(c * IP_SUB, (c + 1) * IP_SUB)
        acc = _dot(h, w_ref[:, cs]) + b_ref[:, cs]
        if c < 2:
            ref[...] = acc.reshape(ref.shape)
        else:
            for bb in range(ref.shape[0]):
                ref[bb] = acc[bb * SEQ:(bb + 1) * SEQ, :].T.reshape(ref.shape[1:])
        a_ref[:, cs] = acc.astype(BF16)


def _mod_row_map(prompt, tm):
    if prompt:
        return lambda i, *_: (0, 0, 0)
    return lambda i, *_: (1 + i // (DEC_SEQ // tm), 0, 0)


def _in_proj(x, modl, g1, w_main, b_main, cscale, w_g, b_g, *, col0, prompt):
    m = x.shape[0]
    tm, tn = IP_TM, 1024
    j0 = col0 // tn
    return pl.pallas_call(
        _inproj_kernel,
        grid=(m // tm, (N_MAIN - col0) // tn),
        in_specs=[
            pl.BlockSpec((tm, D_MODEL), lambda i, j: (i, 0)),
            pl.BlockSpec((None, 6, D_MODEL), _mod_row_map(prompt, tm)),
            pl.BlockSpec((1, D_MODEL), lambda i, j: (0, 0)),
            pl.BlockSpec((D_MODEL, tn), lambda i, j: (0, j + j0)),
            pl.BlockSpec((1, tn), lambda i, j: (0, j + j0)),
            pl.BlockSpec((1, tn), lambda i, j: (0, j + j0)),
            pl.BlockSpec((D_MODEL, 128), lambda i, j: (0, 0)),
            pl.BlockSpec((1, 128), lambda i, j: (0, 0)),
        ],
        out_specs=[pl.BlockSpec((tm, tn), lambda i, j: (i, j)),
                   pl.BlockSpec((tm, 128), lambda i, j: (i, 0))],
        out_shape=[jax.ShapeDtypeStruct((m, N_MAIN - col0), BF16), jax.ShapeDtypeStruct((m, 128), F32)],
        scratch_shapes=[pltpu.VMEM((tm, D_MODEL), BF16)],
        compiler_params=_cparams(("arbitrary", "arbitrary"), 48),
        name="in_proj_ctx" if prompt else "in_proj_lat",
    )(x, modl, g1, w_main, b_main, cscale, w_g, b_g)


def _in_proj_kv(x, modl, g1, w_main, b_main, kv_outs, l):
    m = x.shape[0]
    tm = IP_TM
    nbat = tm // SEQ
    kv_specs = [pl.BlockSpec((nbat, None) + o.shape[2:], functools.partial(lambda i, nd: (i, l) + (0,) * nd,
                                                                            nd=o.ndim - 2)) for o in kv_outs]
    outs = pl.pallas_call(
        _inproj_kv_kernel,
        grid=(m // tm,),
        in_specs=[
            pl.BlockSpec((tm, D_MODEL), lambda i: (i, 0)),
            pl.BlockSpec((None, 6, D_MODEL), _mod_row_map(True, tm)),
            pl.BlockSpec((1, D_MODEL), lambda i: (0, 0)),
            pl.BlockSpec((D_MODEL, N_KV), lambda i: (0, 0)),
            pl.BlockSpec((1, N_KV), lambda i: (0, 0)),
        ] + [pl.BlockSpec(memory_space=pl.ANY)] * 4,
        out_specs=[pl.BlockSpec((tm, N_KV), lambda i: (i, 0))] + kv_specs,
        out_shape=[jax.ShapeDtypeStruct((m, N_KV), BF16)]
                  + [jax.ShapeDtypeStruct(o.shape, o.dtype) for o in kv_outs],
        input_output_aliases={5 + c: 1 + c for c in range(4)},
        compiler_params=_cparams(("arbitrary",), 56),
        name="in_proj_kv_ctx",
    )(x, modl, g1, w_main, b_main, *kv_outs)
    return outs[0], outs[1:]


def _half_masked(q, upper):
    lane = lax.broadcasted_iota(I32, q.shape, 1)
    keep = (lane >= 64) if upper else (lane < 64)
    return jnp.where(keep, q, jnp.zeros_like(q))


def _da_block(qm, k_t, va, m_old, acc_old):
    return _sm_block(_dot(qm, k_t), va, m_old, acc_old)


def _sm_block(s, va, m_old, acc_old):
    nl = s.shape[1] // 128
    mx = s[:, 0:128]
    for c in range(1, nl):
        mx = jnp.maximum(mx, s[:, c * 128:(c + 1) * 128])
    m_new = jnp.broadcast_to(mx.max(axis=1, keepdims=True), mx.shape)
    if m_old is not None:
        m_new = jnp.maximum(m_old, m_new)
    e = jnp.concatenate([jnp.exp(s[:, c * 128:(c + 1) * 128] - m_new) for c in range(nl)], axis=1)
    pv = _dot(e.astype(BF16), va)
    if m_old is None:
        return m_new, pv
    alpha = jnp.exp(m_old - m_new)
    return m_new, acc_old * jnp.concatenate([alpha, alpha], axis=1) + pv


def _rope(x, cos, sin):
    lane = lax.broadcasted_iota(I32, x.shape, 1)
    partner = jnp.where((lane & 16) == 0, pltpu.roll(x, 112, 1), pltpu.roll(x, 16, 1))
    return x * cos + partner * sin


def _da_kernel(*refs, hp, s_new, s_cache, kb, rope, lam_init):
    q_ref, k_ref, v_ref = refs[:3]
    pos = 3
    if s_cache:
        ck_ref, cv_ref = refs[pos:pos + 2]
        pos += 2
    if rope:
        cos_ref, sin_ref = refs[pos:pos + 2]
        pos += 2
    lam_ref, sub_ref, o_ref, kt_scr, va_scr, m_scr, acc_scr = refs[pos:pos + 7]
    qi = pl.program_id(2)
    tq = q_ref.shape[0]
    n_new, n_all = s_new // kb, (s_new + s_cache) // kb

    @pl.when(qi == 0)
    def _():
        for hh in range(hp):
            sl = slice(hh * 128, (hh + 1) * 128)
            for j in range(n_new):
                rows = slice(j * kb, (j + 1) * kb)
                kblk = k_ref[rows, sl].astype(F32)
                if rope:
                    kblk = _rope(kblk, cos_ref[rows, :], sin_ref[rows, :])
                kt_scr[hh, j] = kblk.T.astype(BF16)
            va_scr[hh, 0:s_new, 0:128] = v_ref[:, sl]
            for j in range(n_all - n_new):
                kt_scr[hh, n_new + j] = ck_ref[j * kb:(j + 1) * kb, sl].T.astype(BF16)
            if s_cache:
                va_scr[hh, s_new:s_new + s_cache, 0:128] = cv_ref[:, sl].astype(BF16)
            va_scr[hh, :, 128:256] = jnp.ones((s_new + s_cache, 128), BF16)

    lv = lam_ref[...]
    lam = (jnp.exp(jnp.sum(lv[0:1] * lv[1:2], axis=1, keepdims=True))
           - jnp.exp(jnp.sum(lv[2:3] * lv[3:4], axis=1, keepdims=True)) + lam_init)
    for hh in range(hp):
        sl = slice(hh * 128, (hh + 1) * 128)
        q = q_ref[:, sl]
        if rope:
            q_rows = pl.ds(pl.multiple_of(qi * tq, tq), tq)
            q = _rope(q.astype(F32), cos_ref[q_rows, :], sin_ref[q_rows, :]).astype(BF16)
        qms = (_half_masked(q, False), _half_masked(q, True))
        if n_all == 1:
            accs = [_da_block(qm, kt_scr[hh, 0], va_scr[hh], None, None)[1] for qm in qms]
        else:
            m_scr[...] = jnp.full(m_scr.shape, NEG, F32)
            acc_scr[...] = jnp.zeros_like(acc_scr)

            def body(j, carry, hh=hh, qms=qms):
                va = va_scr[hh, pl.ds(pl.multiple_of(j * kb, kb), kb), :]
                k_t = kt_scr[hh, j]
                new = [_da_block(qm, k_t, va, m_scr[mp], acc_scr[mp]) for mp, qm in enumerate(qms)]
                for mp, (m_new, acc) in enumerate(new):
                    m_scr[mp] = m_new
                    acc_scr[mp] = acc
                return carry

            lax.fori_loop(0, n_all, body, 0, unroll=True)
            accs = [acc_scr[0], acc_scr[1]]
        o1 = accs[0][:, :128] / accs[0][:, 128:]
        o2 = accs[1][:, :128] / accs[1][:, 128:]
        o = o1 - lam * o2
        ms = jnp.mean(o * o, axis=1, keepdims=True)
        on = o * lax.rsqrt(ms + EPS) * sub_ref[...] * (1.0 - lam_init)
        o_ref[:, sl] = on.astype(BF16)


def _diff_attention(cols, cache_k, cache_v, rope_tabs, l, da_lam_l, subln_row, lam_init, *, prompt):
    (qa, qt), (ka, kt), (va, vt) = cols["da_q"], cols["da_k"], cols["da_v"]
    m = qa.shape[0]
    if prompt:
        hp, tq, s_new, s_cache, kb = DA_HEADS, SEQ, SEQ, 0, SEQ
        grid = (BATCH, 1, 1)
        in_specs = [
            pl.BlockSpec((SEQ, 512), lambda b, h, qi: (b, qt)),
            pl.BlockSpec((SEQ, 512), lambda b, h, qi: (b, kt)),
            pl.BlockSpec((SEQ, 512), lambda b, h, qi: (b, vt)),
        ]
        args = [qa, ka, va]
        out_spec = pl.BlockSpec((SEQ, 512), lambda b, h, qi: (b, 0))
    else:
        hp, tq, s_new, s_cache, kb = 1, 1024, DEC_SEQ, PAST_LEN, 512
        nq = DEC_SEQ // tq
        grid = (DEC_BATCH, DA_HEADS, nq)
        in_specs = [
            pl.BlockSpec((tq, 128), lambda b, h, qi: (b * nq + qi, 4 * qt + h)),
            pl.BlockSpec((DEC_SEQ, 128), lambda b, h, qi: (b, 4 * kt + h)),
            pl.BlockSpec((DEC_SEQ, 128), lambda b, h, qi: (b, 4 * vt + h)),
            pl.BlockSpec((None, None, PAST_LEN, 128), lambda b, h, qi: (b, l, 0, h)),
            pl.BlockSpec((None, None, PAST_LEN, 128), lambda b, h, qi: (b, l, 0, h)),
            pl.BlockSpec((DEC_SEQ, 128), lambda b, h, qi: (0, 0)),
            pl.BlockSpec((DEC_SEQ, 128), lambda b, h, qi: (0, 0)),
        ]
        args = [qa, ka, va, cache_k, cache_v, *rope_tabs]
        out_spec = pl.BlockSpec((tq, 128), lambda b, h, qi: (b * nq + qi, h))
    in_specs += [
        pl.BlockSpec((4, DA_QK), lambda b, h, qi: (0, 0)),
        pl.BlockSpec((1, 128), lambda b, h, qi: (0, 0)),
    ]
    args += [da_lam_l, subln_row]
    s_all = s_new + s_cache
    return pl.pallas_call(
        functools.partial(_da_kernel, hp=hp, s_new=s_new, s_cache=s_cache, kb=kb, rope=not prompt,
                          lam_init=lam_init),
        grid=grid,
        in_specs=in_specs,
        out_specs=out_spec,
        out_shape=jax.ShapeDtypeStruct((m, 512), BF16),
        scratch_shapes=[pltpu.VMEM((hp, s_all // kb, 128, kb), BF16), pltpu.VMEM((hp, s_all, 256), BF16),
                        pltpu.VMEM((2, tq, 128), F32), pltpu.VMEM((2, tq, 256), F32)],
        compiler_params=_cparams(("arbitrary", "arbitrary", "arbitrary"), 48),
        name="diff_attn_ctx" if prompt else "diff_attn_lat",
    )(*args)


def _na_bias_kernel(rpb_ref, o_ref):
    h = pl.program_id(0)
    n_dc = 2 * NA_WIN_COLS - 1
    n_dr = 2 * NA_WIN_ROWS - 1
    qc = lax.broadcasted_iota(I32, (GRID_W, 128), 0)
    lane = lax.broadcasted_iota(I32, (GRID_W, 128), 1)
    kc = lane & (GRID_W - 1)
    dcm = kc - qc + (NA_WIN_COLS - 1)
    cstart = jnp.clip(qc - NA_WIN_COLS // 2, 0, GRID_W - NA_WIN_COLS)
    left = lane < GRID_W
    base = h * (n_dr * n_dc)
    for e in range(16):
        acc = jnp.zeros((GRID_W, 128), F32)
        for dc in range(n_dc):
            lv = rpb_ref[base + (e - 1) * n_dc + dc] if e >= 1 else 0.0
            rv = rpb_ref[base + e * n_dc + dc] if e < n_dr else 0.0
            acc = jnp.where(dcm == dc, jnp.where(left, lv, rv), acc)
        o_ref[0, e] = jnp.where(kc >= cstart, jnp.where(kc < cstart + NA_WIN_COLS, acc, NEG), NEG)


def _na_bias_table(rpb_l):
    return pl.pallas_call(
        _na_bias_kernel,
        grid=(NA_HEADS,),
        in_specs=[pl.BlockSpec(memory_space=pltpu.SMEM)],
        out_specs=pl.BlockSpec((1, 16, GRID_W, 128), lambda h: (h, 0, 0, 0)),
        out_shape=jax.ShapeDtypeStruct((NA_HEADS, 16, GRID_W, 128), F32),
        compiler_params=_cparams(("arbitrary",), 16),
        name="na_bias_table",
    )(rpb_l.reshape(-1))


def _na_lat_kernel(q_ref, k_ref, v_ref, ck_ref, cv_ref, bp_ref, o_ref):
    rb = pl.program_id(1)
    ws = jnp.clip(2 * rb - 1, 0, 12)
    delta = 4 * ws - 8 * rb
    tok0 = pl.multiple_of(ws * 256, 256)
    qr = 8 * rb + (lax.broadcasted_iota(I32, (512, 1024), 0) >> 6)
    kr = 4 * ws + (lax.broadcasted_iota(I32, (512, 1024), 1) >> 6)
    st = jnp.clip(qr - NA_WIN_ROWS // 2, 0, GRID_W - NA_WIN_ROWS)
    rowmask = jnp.where(kr >= st, jnp.where(kr < st + NA_WIN_ROWS, 0.0, NEG), NEG)
    ones_b = jnp.ones((1024 + PAST_LEN, 128), BF16)
    lane = lax.broadcasted_iota(I32, (512, 128), 1)
    for g in range(NA_HEADS // 2):
        sl = slice(g * 128, (g + 1) * 128)
        q2 = q_ref[:, sl]
        kall = jnp.concatenate([k_ref[pl.ds(tok0, 1024), sl].astype(F32), ck_ref[:, sl]], axis=0)
        k_t = kall.T.astype(BF16)
        vall = jnp.concatenate([v_ref[pl.ds(tok0, 1024), sl], cv_ref[:, sl].astype(BF16)], axis=0)
        va = jnp.concatenate([vall, ones_b], axis=1)
        outs = []
        for par in range(2):
            h = 2 * g + par
            s = _dot(_half_masked(q2, par == 1), k_t)
            rows = []
            for qrl in range(8):
                tiles = [bp_ref[h, jnp.clip(delta + 2 * jk - qrl + 8, 0, 15)] for jk in range(8)]
                rows.append(jnp.concatenate(tiles, axis=1))
            bias = jnp.concatenate(rows, axis=0)
            s = jnp.concatenate([s[:, :1024] + bias + rowmask, s[:, 1024:]], axis=1)
            acc = _sm_block(s, va, None, None)[1]
            outs.append(acc[:, :128] / acc[:, 128:])
        o_ref[:, sl] = jnp.where(lane < 64, outs[0], outs[1]).astype(BF16)


def _na_lat(cols, cache_k, cache_v, l, bias_tab):
    (qa, qt), (ka, kt), (va, vt) = cols["na_q"], cols["na_k"], cols["na_v"]
    return pl.pallas_call(
        _na_lat_kernel,
        grid=(DEC_BATCH, 8),
        in_specs=[
            pl.BlockSpec((512, 512), lambda b, rb: (b * 8 + rb, qt)),
            pl.BlockSpec((DEC_SEQ, 512), lambda b, rb: (b, kt)),
            pl.BlockSpec((DEC_SEQ, 512), lambda b, rb: (b, vt)),
            pl.BlockSpec((None, None, PAST_LEN, 512), lambda b, rb: (b, l, 0, 0)),
            pl.BlockSpec((None, None, PAST_LEN, 512), lambda b, rb: (b, l, 0, 0)),
            pl.BlockSpec((NA_HEADS, 16, GRID_W, 128), lambda b, rb: (0, 0, 0, 0)),
        ],
        out_specs=pl.BlockSpec((512, 512), lambda b, rb: (b * 8 + rb, 0)),
        out_shape=jax.ShapeDtypeStruct((DEC_BATCH * DEC_SEQ, 512), BF16),
        compiler_params=_cparams(("arbitrary", "arbitrary"), 56),
        name="nbr_attn_lat",
    )(qa, ka, va, cache_k, cache_v, bias_tab)


def _na_ctx_kernel(q_ref, k_ref, v_ref, o_ref):
    ones_b = jnp.ones((q_ref.shape[0], 128), BF16)
    lane = lax.broadcasted_iota(I32, (q_ref.shape[0], 128), 1)
    for g in range(NA_HEADS // 2):
        sl = slice(g * 128, (g + 1) * 128)
        q2 = q_ref[:, sl]
        k_t = k_ref[:, sl].astype(F32).T.astype(BF16)
        va = jnp.concatenate([v_ref[:, sl], ones_b], axis=1)
        acc_a = _da_block(_half_masked(q2, False), k_t, va, None, None)[1]
        acc_b = _da_block(_half_masked(q2, True), k_t, va, None, None)[1]
        o_pair = jnp.where(lane < 64, acc_a[:, :128] / acc_a[:, 128:], acc_b[:, :128] / acc_b[:, 128:])
        o_ref[:, sl] = o_pair.astype(BF16)


def _na_ctx(cols):
    (qa, qt), (ka, kt), (va, vt) = cols["na_q"], cols["na_k"], cols["na_v"]
    return pl.pallas_call(
        _na_ctx_kernel,
        grid=(BATCH,),
        in_specs=[
            pl.BlockSpec((SEQ, 512), lambda b: (b, qt)),
            pl.BlockSpec((SEQ, 512), lambda b: (b, kt)),
            pl.BlockSpec((SEQ, 512), lambda b: (b, vt)),
        ],
        out_specs=pl.BlockSpec((SEQ, 512), lambda b: (b, 0)),
        out_shape=jax.ShapeDtypeStruct((BATCH * SEQ, 512), BF16),
        compiler_params=_cparams(("arbitrary",), 32),
        name="nbr_attn_ctx",
    )(qa, ka, va)


def _log_sigmoid(x):
    return jnp.minimum(x, 0.0) - jnp.log1p(jnp.exp(-jnp.abs(x)))


def _split3(x):
    x1 = x.astype(BF16)
    r1 = x - x1.astype(F32)
    x2 = r1.astype(BF16)
    x3 = (r1 - x2.astype(F32)).astype(BF16)
    return x1, x2, x3


def _ml_kernel(*refs, t_len, nb, unroll, zero_init, emit_state):
    q_ref, k_ref, v_ref, og_ref, g_ref = refs[:5]
    pos = 5
    if not zero_init:
        c0_ref, n0_ref, m0_ref = refs[pos:pos + 3]
        pos += 3
    nrm_ref = refs[pos]
    pos += 2 if emit_state else 1
    o_ref = refs[pos]
    pos += 1
    if emit_state:
        cst_ref, nst_ref, mst_ref = refs[pos:pos + 3]
        pos += 3
    hsum_scr, colb_scr, cmb_scr, rowt_scr, s_scr, m_scr = refs[pos:pos + 6]
    step = pl.program_id(0)
    lc = ML_LC
    nc = t_len // lc
    chains = [(bi, h, d) for bi in range(nb) for h in range(ML_HEADS) for d in range(2)]

    ri = lax.broadcasted_iota(I32, (lc, lc), 0)
    ci = lax.broadcasted_iota(I32, (lc, lc), 1)
    lo = ri >= ci
    up = ri <= ci
    lo_b = jnp.where(lo, 1.0, 0.0).astype(BF16)
    up_b = jnp.where(up, 1.0, 0.0).astype(BF16)
    lane = lax.broadcasted_iota(I32, (lc, 128), 1)
    trow = lax.broadcasted_iota(I32, (lc, 128), 0)
    is_forget = (lane & 1) == 1
    is_bwd = (lane & 2) == 2

    def tri_left(tri, x):
        return sum(_dot(tri, p) for p in _split3(x))

    def prep(c, carry):
        r0 = pl.multiple_of(c * lc, lc)
        g = g_ref[pl.ds(r0, lc), :]
        lf = _log_sigmoid(g)
        cb = jnp.where(is_forget, jnp.where(is_bwd, tri_left(up_b, lf), tri_left(lo_b, lf)), g)
        colb_scr[pl.ds(r0, lc), :] = cb
        ar = pltpu.roll(cb, 127, 1) - cb
        rowt_scr[c] = ar.T[0:16]
        pre = suf = -ar
        k = 1
        while k < lc:
            pre = jnp.maximum(pre, jnp.where(trow >= k, pltpu.roll(pre, k, 0), -jnp.inf))
            suf = jnp.maximum(suf, jnp.where(trow < lc - k, pltpu.roll(suf, lc - k, 0), -jnp.inf))
            k *= 2
        cmb_scr[pl.ds(r0, lc), :] = jnp.where(is_bwd, suf, pre)
        return carry

    lax.fori_loop(0, nb * nc, prep, 0, unroll=2)

    hsum_scr[...] = jnp.zeros_like(hsum_scr)
    if zero_init:
        s_scr[...] = jnp.zeros_like(s_scr)
        m_scr[...] = jnp.zeros_like(m_scr)
    else:
        for idx, (bi, h, d) in enumerate(chains):
            n_rep = jnp.broadcast_to(n0_ref[d, h:h + 1, :], (ML_DIM, ML_DIM)).T
            s_scr[idx * ML_DIM:(idx + 1) * ML_DIM, :] = jnp.concatenate([c0_ref[d, h], n_rep], axis=1)
            m0 = m0_ref[((step * nb + bi) * 2 + d) * ML_HEADS + h]
            m_scr[idx:idx + 1, :] = jnp.full((1, 128), m0, F32)

    ones_b = jnp.ones((lc, ML_DIM), BF16)
    n_ch = len(chains)

    def body(i, carry):
        cs = [i if d == 0 else nc - 1 - i for (_, _, d) in chains]
        r0s = [pl.multiple_of(bi * t_len + c * lc, lc) for (bi, _, _), c in zip(chains, cs)]
        hsl = [slice(h * 128, (h + 1) * 128) for (_, h, _) in chains]
        qs = [q_ref[pl.ds(r0, lc), sl] for r0, sl in zip(r0s, hsl)]
        ks = [k_ref[pl.ds(r0, lc), sl] for r0, sl in zip(r0s, hsl)]
        vs = [v_ref[pl.ds(r0, lc), sl] for r0, sl in zip(r0s, hsl)]
        s_old = [s_scr[idx * ML_DIM:(idx + 1) * ML_DIM, :] for idx in range(n_ch)]
        qk = [_dot_nt(q, k) for q, k in zip(qs, ks)]

        lhs, mts, bcs, ics, mms = [], [], [], [], []
        for idx, (bi, h, d) in enumerate(chains):
            l0 = h * 4 + d * 2
            cb = colb_scr[pl.ds(r0s[idx], lc), :]
            ic = cb[:, l0:l0 + 1]
            bc = cb[:, l0 + 1:l0 + 2]
            arow = rowt_scr[bi * nc + cs[idx], l0:l0 + 1, :]
            mm = m_scr[idx:idx + 1, 0:1]
            cm = cmb_scr[pl.ds(r0s[idx], lc), :][:, l0:l0 + 1]
            gmax = jnp.maximum(mm, cm)
            mt = bc + gmax
            sc = qk[idx] * jnp.exp(jnp.where(lo if d == 0 else up, -(gmax + arow), -jnp.inf))
            inter = jnp.exp(mm - gmax)
            lhs.append(jnp.concatenate([(inter * qs[idx].astype(F32)).astype(BF16), sc.astype(BF16)], axis=1))
            mts.append(mt)
            bcs.append(bc)
            ics.append(ic)
            mms.append(mm)

        nds = [_dot(a, jnp.concatenate([so.astype(BF16), jnp.concatenate([v, ones_b], axis=1)], axis=0))
               for a, so, v in zip(lhs, s_old, vs)]

        hcs, wvs, decays, m_news = [], [], [], []
        for idx, (bi, h, d) in enumerate(chains):
            last = lc - 1 if d == 0 else 0
            nd = nds[idx]
            hcs.append(nd[:, :ML_DIM] / jnp.maximum(jnp.abs(nd[:, ML_DIM:]), jnp.exp(-mts[idx])))
            m_new = mts[idx][last:last + 1, :]
            bl = bcs[idx][last:last + 1, :]
            w = jnp.exp(bl - bcs[idx] + ics[idx] - m_new)
            decays.append(jnp.exp(bl + mms[idx] - m_new))
            m_news.append(jnp.broadcast_to(m_new, (1, 128)))
            wvs.append(jnp.concatenate([(w * vs[idx].astype(F32)).astype(BF16),
                                        jnp.broadcast_to(w, (lc, ML_DIM)).astype(BF16)], axis=1))
        kts = [k.astype(F32).T.astype(BF16) for k in ks]
        kv = [_dot(kt, wv) for kt, wv in zip(kts, wvs)]
        s_scr[...] = jnp.concatenate([dc * so + x for dc, so, x in zip(decays, s_old, kv)], axis=0)
        m_scr[...] = jnp.concatenate(m_news, axis=0)
        for bi in range(nb):
            for d in range(2):
                sel = [idx for idx, ch in enumerate(chains) if ch[0] == bi and ch[2] == d]
                hsum_scr[pl.ds(r0s[sel[0]], lc), :] += jnp.concatenate([hcs[idx] for idx in sel], axis=1)
        return carry

    lax.fori_loop(0, nc, body, 0, unroll=unroll)

    for bi in range(nb):
        rows = slice(bi * t_len, (bi + 1) * t_len)
        for h in range(ML_HEADS):
            sl = slice(h * 128, (h + 1) * 128)
            hs = hsum_scr[rows, sl]
            ms = jnp.mean(hs * hs, axis=-1, keepdims=True)
            y = hs * lax.rsqrt(ms + EPS) * nrm_ref[:, sl]
            o_ref[rows, sl] = (y * jax.nn.sigmoid(og_ref[rows, sl].astype(F32))).astype(BF16)
    if emit_state:
        for idx, (bi, h, d) in enumerate(chains):
            st = s_scr[idx * ML_DIM:(idx + 1) * ML_DIM, :]
            cst_ref[bi, d, h] = st[:, :ML_DIM]
            nst_ref[bi, d, h:h + 1, :] = st[:, ML_DIM:].T[0:1, :]
            mst_ref[bi, d, h:h + 1, :] = m_scr[idx:idx + 1, :]


def _mlstm(cols, gates, states, c_out, l, ml_norm_l, *, prompt):
    names = ("ml_q", "ml_k", "ml_v", "ml_o")
    m = cols["ml_q"][0].shape[0]
    t_len = SEQ if prompt else DEC_SEQ
    nb = 2 if prompt else 1
    n_seq = m // t_len
    rows = nb * t_len
    big = {} if prompt else dict(pipeline_mode=pl.Buffered(1))
    in_specs = [pl.BlockSpec((rows, 512), functools.partial(lambda i, t: (i, t), t=cols[n][1]), **big)
                for n in names]
    in_specs.append(pl.BlockSpec((rows, 128), lambda i: (i, 0)))
    args = [cols[n][0] for n in names] + [gates]
    if not prompt:
        c0, n0, m0 = states
        in_specs += [
            pl.BlockSpec((None, None, 2, ML_HEADS, ML_DIM, ML_DIM), lambda i: (i, l, 0, 0, 0, 0)),
            pl.BlockSpec((None, None, 2, ML_HEADS, ML_DIM), lambda i: (i, l, 0, 0, 0)),
            pl.BlockSpec(memory_space=pltpu.SMEM),
        ]
        args += [c0, n0, m0[:, l].reshape(-1)]
    in_specs.append(pl.BlockSpec((1, 512), lambda i: (0, 0)))
    args.append(ml_norm_l.reshape(1, 512))
    out_shape = [jax.ShapeDtypeStruct((m, 512), BF16)]
    out_specs = [pl.BlockSpec((rows, 512), lambda i: (i, 0))]
    aliases = {}
    if prompt:
        aliases = {len(args): 1}
        in_specs.append(pl.BlockSpec(memory_space=pl.ANY))
        args.append(c_out)
        out_shape += [
            jax.ShapeDtypeStruct(c_out.shape, F32),
            jax.ShapeDtypeStruct((n_seq, 2, ML_HEADS, ML_DIM), F32),
            jax.ShapeDtypeStruct((n_seq, 2, ML_HEADS, 128), F32),
        ]
        out_specs += [
            pl.BlockSpec((nb, None, 2, ML_HEADS, ML_DIM, ML_DIM), lambda i: (i, l, 0, 0, 0, 0)),
            pl.BlockSpec((nb, 2, ML_HEADS, ML_DIM), lambda i: (i, 0, 0, 0)),
            pl.BlockSpec((nb, 2, ML_HEADS, 128), lambda i: (i, 0, 0, 0)),
        ]
    n_chain = nb * ML_HEADS * 2
    return pl.pallas_call(
        functools.partial(_ml_kernel, t_len=t_len, nb=nb, unroll=2,
                          zero_init=prompt, emit_state=prompt),
        grid=(n_seq // nb,),
        in_specs=in_specs,
        out_specs=out_specs,
        out_shape=out_shape,
        input_output_aliases=aliases,
        scratch_shapes=[pltpu.VMEM((rows, 512), F32), pltpu.VMEM((rows, 128), F32), pltpu.VMEM((rows, 128), F32),
                        pltpu.VMEM((rows // ML_LC, 16, ML_LC), F32),
                        pltpu.VMEM((n_chain * ML_DIM, 2 * ML_DIM), F32), pltpu.VMEM((n_chain, 128), F32)],
        compiler_params=_cparams(("arbitrary",), 56),
        name="mlstm_ctx" if prompt else "mlstm_lat",
    )(*args)


def _merge_kernel(x_ref, mod_ref, oda_ref, oml_ref, ona_ref, g0_ref, g1_ref, g2_ref,
                  wda_ref, wml_ref, wna_ref, wo_ref, o_ref):
    def branch(o, w, g):
        return jax.nn.sigmoid(g[...].astype(F32)) * _dot(o[...], w[...])

    merged = (branch(oda_ref, wda_ref, g0_ref) + branch(oml_ref, wml_ref, g1_ref)
              + branch(ona_ref, wna_ref, g2_ref))
    o_ref[...] = x_ref[...] + mod_ref[2:3, :] * _dot(merged.astype(BF16), wo_ref[...])


def _merge_out(x, modl, o_da, o_ml, o_na, cols, w_da, w_ml, w_na, w_out, *, prompt):
    m = x.shape[0]
    tm = 512
    const = lambda i: (0, 0)
    a, gt = cols["merge"]
    g0 = gt // 2
    return pl.pallas_call(
        _merge_kernel,
        grid=(m // tm,),
        in_specs=[
            pl.BlockSpec((tm, D_MODEL), lambda i: (i, 0)),
            pl.BlockSpec((None, 6, D_MODEL), _mod_row_map(prompt, tm)),
            pl.BlockSpec((tm, 512), lambda i: (i, 0)),
            pl.BlockSpec((tm, 512), lambda i: (i, 0)),
            pl.BlockSpec((tm, 512), lambda i: (i, 0)),
            pl.BlockSpec((tm, D_MODEL), lambda i: (i, g0)),
            pl.BlockSpec((tm, D_MODEL), lambda i: (i, g0 + 1)),
            pl.BlockSpec((tm, D_MODEL), lambda i: (i, g0 + 2)),
            pl.BlockSpec((512, D_MODEL), const),
            pl.BlockSpec((512, D_MODEL), const),
            pl.BlockSpec((512, D_MODEL), const),
            pl.BlockSpec((D_MODEL, D_MODEL), const),
        ],
        out_specs=pl.BlockSpec((tm, D_MODEL), lambda i: (i, 0)),
        out_shape=jax.ShapeDtypeStruct((m, D_MODEL), F32),
        compiler_params=_cparams(("arbitrary",), 48),
        name="merge_out_ctx" if prompt else "merge_out_lat",
    )(x, modl, o_da, o_ml, o_na, a, a, a, w_da, w_ml, w_na, w_out)


def _ffn_kernel(x_ref, mod_ref, g_ref, w1_ref, b1_ref, w2_ref, b2_ref, *refs, final):
    o_ref, h_scr, acc_scr = refs[-3:]
    f = pl.program_id(1)

    @pl.when(f == 0)
    def _():
        h_scr[...] = _modulated_norm(x_ref[...], g_ref[...], mod_ref[3:4, :], mod_ref[4:5, :]).astype(BF16)
        acc_scr[...] = jnp.zeros_like(acc_scr)

    h = h_scr[...]
    part = None
    for c in range(w1_ref.shape[1] // IP_SUB):
        cs = slice(c * IP_SUB, (c + 1) * IP_SUB)
        u = jnp.maximum(_dot(h, w1_ref[:, cs]) + b1_ref[:, cs], 0.0)
        d = _dot((u * u).astype(BF16), w2_ref[cs, :])
        part = d if part is None else part + d
    acc_scr[...] += part

    @pl.when(f == pl.num_programs(1) - 1)
    def _():
        y = x_ref[...] + mod_ref[5:6, :] * (acc_scr[...] + b2_ref[...])
        if final:
            ms = jnp.mean(y * y, axis=-1, keepdims=True)
            y = y * lax.rsqrt(ms + EPS) * refs[0][...]
        o_ref[...] = y


def _ffn(x, modl, g2, w1, b1, w2, b2, g_final, *, prompt):
    m = x.shape[0]
    tm, tf = 1024, 1024
    final = g_final is not None
    in_specs = [
        pl.BlockSpec((tm, D_MODEL), lambda i, f: (i, 0)),
        pl.BlockSpec((None, 6, D_MODEL), _mod_row_map(prompt, tm)),
        pl.BlockSpec((1, D_MODEL), lambda i, f: (0, 0)),
        pl.BlockSpec((D_MODEL, tf), lambda i, f: (0, f)),
        pl.BlockSpec((1, tf), lambda i, f: (0, f)),
        pl.BlockSpec((tf, D_MODEL), lambda i, f: (f, 0)),
        pl.BlockSpec((1, D_MODEL), lambda i, f: (0, 0)),
    ]
    args = [x, modl, g2, w1, b1, w2, b2]
    if final:
        in_specs.append(pl.BlockSpec((1, D_MODEL), lambda i, f: (0, 0)))
        args.append(g_final)
    return pl.pallas_call(
        functools.partial(_ffn_kernel, final=final),
        grid=(m // tm, D_FF // tf),
        in_specs=in_specs,
        out_specs=pl.BlockSpec((tm, D_MODEL), lambda i, f: (i, 0)),
        out_shape=jax.ShapeDtypeStruct((m, D_MODEL), F32),
        scratch_shapes=[pltpu.VMEM((tm, D_MODEL), BF16), pltpu.VMEM((tm, D_MODEL), F32)],
        compiler_params=_cparams(("arbitrary", "arbitrary"), 48),
        name="ffn_ctx" if prompt else "ffn_lat",
    )(*args)


def _rope_tables():
    t = jnp.arange(DEC_SEQ)
    lane = jnp.arange(128)
    sub = lane % 32
    freq = ROPE_BASE ** (-(2 * (sub % 16)).astype(F32) / 32.0)
    use_row = (lane % 64) < 32
    posv = jnp.where(use_row[None, :], (t // GRID_W)[:, None], (t % GRID_W)[:, None]).astype(F32)
    ang = posv * freq[None, :]
    sign = jnp.where(sub < 16, -1.0, 1.0).astype(F32)
    return jnp.cos(ang), jnp.sin(ang) * sign[None, :]


def kernel(x_prompt, x_sample, cache_da_k, cache_da_v, cache_na_k, cache_na_v, state_ml_C, state_ml_n,
           state_ml_m, c, c_ctx, w_mod, b_mod, norm1, w_in, b_in, da_lam, da_subln, ml_norm, na_rpb,
           w_up_da, w_up_ml, w_up_na, w_out, norm2, w_ff1, b_ff1, w_ff2, b_ff2, norm_f):
    xp = x_prompt.reshape(BATCH * SEQ, D_MODEL)
    xs = x_sample.reshape(DEC_BATCH * DEC_SEQ, D_MODEL)

    w_main, w_g = _stage_w_in(jnp.swapaxes(w_in, 1, 2))
    b_main = jnp.concatenate([b_in[..., a:b] for a, b in W_SEGS], axis=-1).reshape(DEPTH, 1, N_MAIN)
    tile_of = {n: t for t, n in enumerate(W_TILES)}
    src, dst = [], []
    for hh in range(ML_HEADS):
        for dr in range(2):
            for tt in range(2):
                src.append(GATE_OFF + dr * 2 * ML_HEADS + tt * ML_HEADS + hh)
                dst.append(hh * 4 + dr * 2 + tt)
    src, dst = np.array(src), np.array(dst)
    b_g = jnp.zeros((DEPTH, 1, 128), F32).at[:, 0, dst].set(b_in[:, src])
    cs = np.ones((1, N_MAIN), np.float32)
    for name, scale in (("da_q", DA_QK ** -0.5), ("ml_k", ML_DIM ** -0.5), ("na_q", NA_DIM ** -0.5)):
        cs[0, tile_of[name] * 512:(tile_of[name] + 1) * 512] = scale
    cscale = jnp.asarray(cs)
    w_da_b, w_ml_b, w_na_b, w_out_b = (w.astype(BF16) for w in (w_up_da, w_up_ml, w_up_na, w_out))
    w_ff1_b, w_ff2_b = w_ff1.astype(BF16), w_ff2.astype(BF16)
    rope_tabs = _rope_tables()

    cc = jnp.zeros((8, D_MODEL), F32).at[0].set(c_ctx).at[1:1 + DEC_BATCH].set(c)
    mod = _modulation(cc, w_mod, b_mod).reshape(DEPTH, 8, 6, D_MODEL)

    cdk = cache_da_k.reshape(DEC_BATCH, DEPTH, PAST_LEN, 512)
    cdv = cache_da_v.reshape(DEC_BATCH, DEPTH, PAST_LEN, 512)
    cnk = cache_na_k.reshape(DEC_BATCH, DEPTH, PAST_LEN, 512)
    cnv = cache_na_v.reshape(DEC_BATCH, DEPTH, PAST_LEN, 512)

    kv_outs = [jnp.zeros((BATCH, DEPTH, SEQ, DA_HEADS, 2 * DA_QK), F32) for _ in range(2)]
    kv_outs += [jnp.zeros((BATCH, DEPTH, NA_HEADS, NA_DIM, SEQ), F32) for _ in range(2)]
    c_out = jnp.zeros((BATCH, DEPTH, 2, ML_HEADS, ML_DIM, ML_DIM), F32)
    coll_n, coll_m = [], []
    gf = norm_f.reshape(1, D_MODEL)
    for l in range(DEPTH):
        lam_init = 0.8 - 0.6 * math.exp(-0.3 * l)
        modl = mod[l]
        g1 = norm1[l].reshape(1, D_MODEL)
        g2 = norm2[l].reshape(1, D_MODEL)
        sub_row = da_subln[l].reshape(1, 128)
        bias_tab = _na_bias_table(na_rpb[l])
        for prompt in (True, False):
            x = xp if prompt else xs
            if prompt:
                a_kv, kv_outs = _in_proj_kv(x, modl, g1, w_main[l], b_main[l], kv_outs, l)
                a, gates = _in_proj(x, modl, g1, w_main[l], b_main[l], cscale, w_g[l], b_g[l],
                                    col0=N_KV, prompt=True)
                cols = {n: (a_kv, t) if t < 4 else (a, t - 4) for n, t in tile_of.items()}
            else:
                a, gates = _in_proj(x, modl, g1, w_main[l], b_main[l], cscale, w_g[l], b_g[l],
                                    col0=0, prompt=False)
                cols = {n: (a, t) for n, t in tile_of.items()}
            o_da = _diff_attention(cols, cdk, cdv, rope_tabs, l, da_lam[l], sub_row, lam_init, prompt=prompt)
            ml_out = _mlstm(cols, gates, (state_ml_C, state_ml_n, state_ml_m), c_out, l, ml_norm[l],
                            prompt=prompt)
            o_ml = ml_out[0]
            if prompt:
                c_out = ml_out[1]
                coll_n.append(ml_out[2])
                coll_m.append(ml_out[3][..., 0])
                o_na = _na_ctx(cols)
            else:
                o_na = _na_lat(cols, cnk, cnv, l, bias_tab)
            x = _merge_out(x, modl, o_da, o_ml, o_na, cols, w_da_b[l], w_ml_b[l], w_na_b[l], w_out_b[l],
                           prompt=prompt)
            x = _ffn(x, modl, g2, w_ff1_b[l], b_ff1[l].reshape(1, D_FF), w_ff2_b[l],
                     b_ff2[l].reshape(1, D_MODEL), gf if l == DEPTH - 1 else None, prompt=prompt)
            if prompt:
                xp = x
            else:
                xs = x

    y_prompt = xp.reshape(BATCH, SEQ, D_MODEL)
    y_sample = xs.reshape(DEC_BATCH, DEC_SEQ, D_MODEL)
    new_da_k, new_da_v = kv_outs[0], kv_outs[1]
    new_na_k = jnp.transpose(kv_outs[2], (0, 1, 4, 2, 3))
    new_na_v = jnp.transpose(kv_outs[3], (0, 1, 4, 2, 3))
    new_ml_n = jnp.stack(coll_n, axis=1)
    new_ml_m = jnp.stack(coll_m, axis=1)
    return (y_prompt, y_sample, new_da_k, new_da_v, new_na_k, new_na_v, c_out, new_ml_n, new_ml_m)
```

```python
import functools
import math

import jax
import jax.numpy as jnp
import numpy as np
from jax import lax
from jax.experimental import pallas as pl
from jax.experimental.pallas import tpu as pltpu

F32 = jnp.float32
BF16 = jnp.bfloat16
I32 = jnp.int32

D_MODEL = 1024
BATCH = 32
SEQ = 256
DEPTH = 4
DEC_BATCH = 2
DEC_SEQ = 4096
PAST_LEN = 512
GRID_W = 64
DA_HEADS = 4
DA_QK = 64
ML_HEADS = 4
ML_DIM = 128
ML_LC = 256
NA_HEADS = 8
NA_DIM = 64
NA_WIN_ROWS = 8
NA_WIN_COLS = 16
D_FF = 4 * D_MODEL
ROPE_BASE = 10000.0
EPS = 1e-6
N_MAIN = 8192
GATE_OFF = 3584
NEG = -1e30

MIB = 1024 * 1024
NT_DIMS = (((1,), (1,)), ((), ()))


def _cparams(sem, vmem_mib):
    return pltpu.CompilerParams(dimension_semantics=sem, vmem_limit_bytes=vmem_mib * MIB)


def _dot(a, b):
    return jnp.dot(a, b, preferred_element_type=F32)


def _dot_nt(a, b):
    return lax.dot_general(a, b, NT_DIMS, preferred_element_type=F32)


W_SEGS = ((512, 1024), (1024, 1536), (4112, 4624), (4624, 5136), (0, 512), (1536, GATE_OFF),
          (GATE_OFF + 16, 4112), (5136, 8208))
W_TILES = ("da_k", "da_v", "na_k", "na_v", "da_q", "ml_q", "ml_k", "ml_v", "ml_o", "na_q", "merge")


def _stage_w_kernel(w_ref, o_ref, g_ref):
    off = 0
    for a, b in W_SEGS:
        for s in range(a, b, IP_SUB):
            o_ref[0, :, off:off + IP_SUB] = w_ref[0, s:s + IP_SUB, :].T.astype(BF16)
            off += IP_SUB
    gates = w_ref[0, GATE_OFF:GATE_OFF + 16, :]
    r = lax.broadcasted_iota(I32, (128, gates.shape[1]), 0)
    g_t = jnp.zeros((128, gates.shape[1]), F32)
    for s in range(16):
        dst = (s & 3) * 4 + (s >> 3) * 2 + ((s >> 2) & 1)
        g_t = jnp.where(r == dst, gates[s:s + 1, :], g_t)
    g_ref[0] = g_t.T.astype(BF16)


def _stage_w_in(w_in_t):
    kt = 256
    n_proj = w_in_t.shape[1]
    return pl.pallas_call(
        _stage_w_kernel,
        grid=(DEPTH, D_MODEL // kt),
        in_specs=[pl.BlockSpec((1, n_proj, kt), lambda l, i: (l, 0, i))],
        out_specs=[pl.BlockSpec((1, kt, N_MAIN), lambda l, i: (l, i, 0)),
                   pl.BlockSpec((1, kt, 128), lambda l, i: (l, i, 0))],
        out_shape=[jax.ShapeDtypeStruct((DEPTH, D_MODEL, N_MAIN), BF16),
                   jax.ShapeDtypeStruct((DEPTH, D_MODEL, 128), BF16)],
        compiler_params=_cparams(("arbitrary", "arbitrary"), 48),
        name="stage_w_in",
    )(w_in_t)


def _mod_kernel(c_ref, w_ref, b_ref, o_ref):
    c = c_ref[...]
    s = (c * jax.nn.sigmoid(c)).astype(BF16)
    o_ref[0] = _dot(s, w_ref[0].astype(BF16)) + b_ref[0]


def _modulation(cc, w_mod, b_mod):
    tn = 1536
    n = 6 * D_MODEL
    return pl.pallas_call(
        _mod_kernel,
        grid=(DEPTH, n // tn),
        in_specs=[
            pl.BlockSpec((8, D_MODEL), lambda l, j: (0, 0)),
            pl.BlockSpec((1, D_MODEL, tn), lambda l, j: (l, 0, j)),
            pl.BlockSpec((1, 1, tn), lambda l, j: (l, 0, j)),
        ],
        out_specs=pl.BlockSpec((1, 8, tn), lambda l, j: (l, 0, j)),
        out_shape=jax.ShapeDtypeStruct((DEPTH, 8, n), F32),
        compiler_params=_cparams(("arbitrary", "arbitrary"), 40),
        name="modulation",
    )(cc, w_mod, b_mod.reshape(DEPTH, 1, n))


def _modulated_norm(x, g, shift, scale):
    ms = jnp.mean(x * x, axis=-1, keepdims=True)
    y = x * lax.rsqrt(ms + EPS) * g
    return y * (1.0 + scale) + shift


IP_TM = 1024
IP_SUB = 512
N_KV = 2048


def _inproj_kernel(x_ref, mod_ref, g_ref, w_ref, b_ref, cs_ref, wg_ref, bg_ref, a_ref, gate_ref, h_scr):
    @pl.when(pl.program_id(1) == 0)
    def _():
        h = _modulated_norm(x_ref[...], g_ref[...], mod_ref[0:1, :], mod_ref[1:2, :]).astype(BF16)
        h_scr[...] = h
        gate_ref[...] = _dot(h, wg_ref[...]) + bg_ref[...]

    h = h_scr[...]
    for c in range(a_ref.shape[1] // IP_SUB):
        cs = slice(c * IP_SUB, (c + 1) * IP_SUB)
        a_ref[:, cs] = ((_dot(h, w_ref[:, cs]) + b_ref[:, cs]) * cs_ref[:, cs]).astype(BF16)


def _inproj_kv_kernel(x_ref, mod_ref, g_ref, w_ref, b_ref, *refs):
    a_ref = refs[4]
    f32_refs = refs[5:9]
    h = _modulated_norm(x_ref[...], g_ref[...], mod_ref[0:1, :], mod_ref[1:2, :]).astype(BF16)
    for c, ref in enumerate(f32_refs):
        cs = slice(c * IP_SUB, (c + 1) * IP_SUB)
        acc = _dot(h, w_ref[:, cs]) + b_ref[:, cs]
        if c < 2:
            ref[...] = acc.reshape(ref.shape)
        else:
            for bb in range(ref.shape[0]):
                ref[bb] = acc[bb * SEQ:(bb + 1) * SEQ, :].T.reshape(ref.shape[1:])
        a_ref[:, cs] = acc.astype(BF16)


def _mod_row_map(prompt, tm):
    if prompt:
        return lambda i, *_: (0, 0, 0)
    return lambda i, *_: (1 + i // (DEC_SEQ // tm), 0, 0)


def _in_proj(x, modl, g1, w_main, b_main, cscale, w_g, b_g, l, *, col0, prompt):
    m = x.shape[0]
    tm, tn = IP_TM, 1024
    j0 = col0 // tn
    return pl.pallas_call(
        _inproj_kernel,
        grid=(m // tm, (N_MAIN - col0) // tn),
        in_specs=[
            pl.BlockSpec((tm, D_MODEL), lambda i, j: (i, 0)),
            pl.BlockSpec((None, 6, D_MODEL), _mod_row_map(prompt, tm)),
            pl.BlockSpec((1, D_MODEL), lambda i, j: (0, 0)),
            pl.BlockSpec((None, D_MODEL, tn), lambda i, j: (l, 0, j + j0)),
            pl.BlockSpec((1, tn), lambda i, j: (0, j + j0)),
            pl.BlockSpec((1, tn), lambda i, j: (0, j + j0)),
            pl.BlockSpec((None, D_MODEL, 128), lambda i, j: (l, 0, 0)),
            pl.BlockSpec((1, 128), lambda i, j: (0, 0)),
        ],
        out_specs=[pl.BlockSpec((tm, tn), lambda i, j: (i, j)),
                   pl.BlockSpec((tm, 128), lambda i, j: (i, 0))],
        out_shape=[jax.ShapeDtypeStruct((m, N_MAIN - col0), BF16), jax.ShapeDtypeStruct((m, 128), F32)],
        scratch_shapes=[pltpu.VMEM((tm, D_MODEL), BF16)],
        compiler_params=_cparams(("arbitrary", "arbitrary"), 48),
        name="in_proj_ctx" if prompt else "in_proj_lat",
    )(x, modl, g1, w_main, b_main, cscale, w_g, b_g)


def _in_proj_kv(x, modl, g1, w_main, b_main, kv_outs, l):
    m = x.shape[0]
    tm = IP_TM
    nbat = tm // SEQ
    kv_specs = [pl.BlockSpec((nbat, None) + o.shape[2:], functools.partial(lambda i, nd: (i, l) + (0,) * nd,
                                                                            nd=o.ndim - 2)) for o in kv_outs]
    outs = pl.pallas_call(
        _inproj_kv_kernel,
        grid=(m // tm,),
        in_specs=[
            pl.BlockSpec((tm, D_MODEL), lambda i: (i, 0)),
            pl.BlockSpec((None, 6, D_MODEL), _mod_row_map(True, tm)),
            pl.BlockSpec((1, D_MODEL), lambda i: (0, 0)),
            pl.BlockSpec((None, D_MODEL, N_KV), lambda i: (l, 0, 0)),
            pl.BlockSpec((1, N_KV), lambda i: (0, 0)),
        ] + [pl.BlockSpec(memory_space=pl.ANY)] * 4,
        out_specs=[pl.BlockSpec((tm, N_KV), lambda i: (i, 0))] + kv_specs,
        out_shape=[jax.ShapeDtypeStruct((m, N_KV), BF16)]
                  + [jax.ShapeDtypeStruct(o.shape, o.dtype) for o in kv_outs],
        input_output_aliases={5 + c: 1 + c for c in range(4)},
        compiler_params=_cparams(("arbitrary",), 56),
        name="in_proj_kv_ctx",
    )(x, modl, g1, w_main, b_main, *kv_outs)
    return outs[0], outs[1:]


def _half_masked(q, upper):
    lane = lax.broadcasted_iota(I32, q.shape, 1)
    keep = (lane >= 64) if upper else (lane < 64)
    return jnp.where(keep, q, jnp.zeros_like(q))


def _da_block(qm, k_t, va, m_old, acc_old):
    return _sm_block(_dot(qm, k_t), va, m_old, acc_old)


def _sm_block(s, va, m_old, acc_old):
    nl = s.shape[1] // 128
    mx = s[:, 0:128]
    for c in range(1, nl):
        mx = jnp.maximum(mx, s[:, c * 128:(c + 1) * 128])
    m_new = jnp.broadcast_to(mx.max(axis=1, keepdims=True), mx.shape)
    if m_old is not None:
        m_new = jnp.maximum(m_old, m_new)
    e = jnp.concatenate([jnp.exp(s[:, c * 128:(c + 1) * 128] - m_new) for c in range(nl)], axis=1)
    pv = _dot(e.astype(BF16), va)
    if m_old is None:
        return m_new, pv
    alpha = jnp.exp(m_old - m_new)
    return m_new, acc_old * jnp.concatenate([alpha, alpha], axis=1) + pv


def _rope(x, cos, sin):
    lane = lax.broadcasted_iota(I32, x.shape, 1)
    partner = jnp.where((lane & 16) == 0, pltpu.roll(x, 112, 1), pltpu.roll(x, 16, 1))
    return x * cos + partner * sin


def _da_kernel(*refs, hp, s_new, s_cache, kb, rope, lam_init):
    q_ref, k_ref, v_ref = refs[:3]
    pos = 3
    if s_cache:
        ck_ref, cv_ref = refs[pos:pos + 2]
        pos += 2
    if rope:
        cos_ref, sin_ref = refs[pos:pos + 2]
        pos += 2
    lam_ref, sub_ref, o_ref, kt_scr, va_scr, m_scr, acc_scr = refs[pos:pos + 7]
    qi = pl.program_id(2)
    tq = q_ref.shape[0]
    n_new, n_all = s_new // kb, (s_new + s_cache) // kb

    @pl.when(qi == 0)
    def _():
        for hh in range(hp):
            sl = slice(hh * 128, (hh + 1) * 128)
            for j in range(n_new):
                rows = slice(j * kb, (j + 1) * kb)
                kblk = k_ref[rows, sl].astype(F32)
                if rope:
                    kblk = _rope(kblk, cos_ref[rows, :], sin_ref[rows, :])
                kt_scr[hh, j] = kblk.T.astype(BF16)
            va_scr[hh, 0:s_new, 0:128] = v_ref[:, sl]
            for j in range(n_all - n_new):
                kt_scr[hh, n_new + j] = ck_ref[j * kb:(j + 1) * kb, sl].T.astype(BF16)
            if s_cache:
                va_scr[hh, s_new:s_new + s_cache, 0:128] = cv_ref[:, sl].astype(BF16)
            va_scr[hh, :, 128:256] = jnp.ones((s_new + s_cache, 128), BF16)

    lv = lam_ref[...]
    lam = (jnp.exp(jnp.sum(lv[0:1] * lv[1:2], axis=1, keepdims=True))
           - jnp.exp(jnp.sum(lv[2:3] * lv[3:4], axis=1, keepdims=True)) + lam_init)
    for hh in range(hp):
        sl = slice(hh * 128, (hh + 1) * 128)
        q = q_ref[:, sl]
        if rope:
            q_rows = pl.ds(pl.multiple_of(qi * tq, tq), tq)
            q = _rope(q.astype(F32), cos_ref[q_rows, :], sin_ref[q_rows, :]).astype(BF16)
        qms = (_half_masked(q, False), _half_masked(q, True))
        if n_all == 1:
            accs = [_da_block(qm, kt_scr[hh, 0], va_scr[hh], None, None)[1] for qm in qms]
        else:
            m_scr[...] = jnp.full(m_scr.shape, NEG, F32)
            acc_scr[...] = jnp.zeros_like(acc_scr)

            def body(j, carry, hh=hh, qms=qms):
                va = va_scr[hh, pl.ds(pl.multiple_of(j * kb, kb), kb), :]
                k_t = kt_scr[hh, j]
                new = [_da_block(qm, k_t, va, m_scr[mp], acc_scr[mp]) for mp, qm in enumerate(qms)]
                for mp, (m_new, acc) in enumerate(new):
                    m_scr[mp] = m_new
                    acc_scr[mp] = acc
                return carry

            lax.fori_loop(0, n_all, body, 0, unroll=True)
            accs = [acc_scr[0], acc_scr[1]]
        o1 = accs[0][:, :128] / accs[0][:, 128:]
        o2 = accs[1][:, :128] / accs[1][:, 128:]
        o = o1 - lam * o2
        ms = jnp.mean(o * o, axis=1, keepdims=True)
        on = o * lax.rsqrt(ms + EPS) * sub_ref[...] * (1.0 - lam_init)
        o_ref[:, sl] = on.astype(BF16)


def _diff_attention(cols, cache_k, cache_v, rope_tabs, l, da_lam_l, subln_row, lam_init, *, prompt):
    (qa, qt), (ka, kt), (va, vt) = cols["da_q"], cols["da_k"], cols["da_v"]
    m = qa.shape[0]
    if prompt:
        hp, tq, s_new, s_cache, kb = DA_HEADS, SEQ, SEQ, 0, SEQ
        grid = (BATCH, 1, 1)
        in_specs = [
            pl.BlockSpec((SEQ, 512), lambda b, h, qi: (b, qt)),
            pl.BlockSpec((SEQ, 512), lambda b, h, qi: (b, kt)),
            pl.BlockSpec((SEQ, 512), lambda b, h, qi: (b, vt)),
        ]
        args = [qa, ka, va]
        out_spec = pl.BlockSpec((SEQ, 512), lambda b, h, qi: (b, 0))
    else:
        hp, tq, s_new, s_cache, kb = 1, 1024, DEC_SEQ, PAST_LEN, 512
        nq = DEC_SEQ // tq
        grid = (DEC_BATCH, DA_HEADS, nq)
        in_specs = [
            pl.BlockSpec((tq, 128), lambda b, h, qi: (b * nq + qi, 4 * qt + h)),
            pl.BlockSpec((DEC_SEQ, 128), lambda b, h, qi: (b, 4 * kt + h)),
            pl.BlockSpec((DEC_SEQ, 128), lambda b, h, qi: (b, 4 * vt + h)),
            pl.BlockSpec((None, None, PAST_LEN, 128), lambda b, h, qi: (b, l, 0, h)),
            pl.BlockSpec((None, None, PAST_LEN, 128), lambda b, h, qi: (b, l, 0, h)),
            pl.BlockSpec((DEC_SEQ, 128), lambda b, h, qi: (0, 0)),
            pl.BlockSpec((DEC_SEQ, 128), lambda b, h, qi: (0, 0)),
        ]
        args = [qa, ka, va, cache_k, cache_v, *rope_tabs]
        out_spec = pl.BlockSpec((tq, 128), lambda b, h, qi: (b * nq + qi, h))
    in_specs += [
        pl.BlockSpec((4, DA_QK), lambda b, h, qi: (0, 0)),
        pl.BlockSpec((1, 128), lambda b, h, qi: (0, 0)),
    ]
    args += [da_lam_l, subln_row]
    s_all = s_new + s_cache
    return pl.pallas_call(
        functools.partial(_da_kernel, hp=hp, s_new=s_new, s_cache=s_cache, kb=kb, rope=not prompt,
                          lam_init=lam_init),
        grid=grid,
        in_specs=in_specs,
        out_specs=out_spec,
        out_shape=jax.ShapeDtypeStruct((m, 512), BF16),
        scratch_shapes=[pltpu.VMEM((hp, s_all // kb, 128, kb), BF16), pltpu.VMEM((hp, s_all, 256), BF16),
                        pltpu.VMEM((2, tq, 128), F32), pltpu.VMEM((2, tq, 256), F32)],
        compiler_params=_cparams(("arbitrary", "arbitrary", "arbitrary"), 48),
        name="diff_attn_ctx" if prompt else "diff_attn_lat",
    )(*args)


def _na_bias_kernel(rpb_ref, o_ref):
    h = pl.program_id(0)
    n_dc = 2 * NA_WIN_COLS - 1
    n_dr = 2 * NA_WIN_ROWS - 1
    qc = lax.broadcasted_iota(I32, (GRID_W, 128), 0)
    lane = lax.broadcasted_iota(I32, (GRID_W, 128), 1)
    kc = lane & (GRID_W - 1)
    dcm = kc - qc + (NA_WIN_COLS - 1)
    cstart = jnp.clip(qc - NA_WIN_COLS // 2, 0, GRID_W - NA_WIN_COLS)
    left = lane < GRID_W
    base = h * (n_dr * n_dc)
    for e in range(16):
        acc = jnp.zeros((GRID_W, 128), F32)
        for dc in range(n_dc):
            lv = rpb_ref[base + (e - 1) * n_dc + dc] if e >= 1 else 0.0
            rv = rpb_ref[base + e * n_dc + dc] if e < n_dr else 0.0
            acc = jnp.where(dcm == dc, jnp.where(left, lv, rv), acc)
        o_ref[0, e] = jnp.where(kc >= cstart, jnp.where(kc < cstart + NA_WIN_COLS, acc, NEG), NEG)


def _na_bias_table(rpb_l):
    return pl.pallas_call(
        _na_bias_kernel,
        grid=(NA_HEADS,),
        in_specs=[pl.BlockSpec(memory_space=pltpu.SMEM)],
        out_specs=pl.BlockSpec((1, 16, GRID_W, 128), lambda h: (h, 0, 0, 0)),
        out_shape=jax.ShapeDtypeStruct((NA_HEADS, 16, GRID_W, 128), F32),
        compiler_params=_cparams(("arbitrary",), 16),
        name="na_bias_table",
    )(rpb_l.reshape(-1))


def _na_lat_kernel(q_ref, k_ref, v_ref, ck_ref, cv_ref, bp_ref, o_ref):
    rb = pl.program_id(1)
    ws = jnp.clip(2 * rb - 1, 0, 12)
    delta = 4 * ws - 8 * rb
    tok0 = pl.multiple_of(ws * 256, 256)
    qr = 8 * rb + (lax.broadcasted_iota(I32, (512, 1024), 0) >> 6)
    kr = 4 * ws + (lax.broadcasted_iota(I32, (512, 1024), 1) >> 6)
    st = jnp.clip(qr - NA_WIN_ROWS // 2, 0, GRID_W - NA_WIN_ROWS)
    rowmask = jnp.where(kr >= st, jnp.where(kr < st + NA_WIN_ROWS, 0.0, NEG), NEG)
    ones_b = jnp.ones((1024 + PAST_LEN, 128), BF16)
    lane = lax.broadcasted_iota(I32, (512, 128), 1)
    for g in range(NA_HEADS // 2):
        sl = slice(g * 128, (g + 1) * 128)
        q2 = q_ref[:, sl]
        kall = jnp.concatenate([k_ref[pl.ds(tok0, 1024), sl].astype(F32), ck_ref[:, sl]], axis=0)
        k_t = kall.T.astype(BF16)
        vall = jnp.concatenate([v_ref[pl.ds(tok0, 1024), sl], cv_ref[:, sl].astype(BF16)], axis=0)
        va = jnp.concatenate([vall, ones_b], axis=1)
        outs = []
        for par in range(2):
            h = 2 * g + par
            s = _dot(_half_masked(q2, par == 1), k_t)
            rows = []
            for qrl in range(8):
                tiles = [bp_ref[h, jnp.clip(delta + 2 * jk - qrl + 8, 0, 15)] for jk in range(8)]
                rows.append(jnp.concatenate(tiles, axis=1))
            bias = jnp.concatenate(rows, axis=0)
            s = jnp.concatenate([s[:, :1024] + bias + rowmask, s[:, 1024:]], axis=1)
            acc = _sm_block(s, va, None, None)[1]
            outs.append(acc[:, :128] / acc[:, 128:])
        o_ref[:, sl] = jnp.where(lane < 64, outs[0], outs[1]).astype(BF16)


def _na_lat(cols, cache_k, cache_v, l, bias_tab):
    (qa, qt), (ka, kt), (va, vt) = cols["na_q"], cols["na_k"], cols["na_v"]
    return pl.pallas_call(
        _na_lat_kernel,
        grid=(DEC_BATCH, 8),
        in_specs=[
            pl.BlockSpec((512, 512), lambda b, rb: (b * 8 + rb, qt)),
            pl.BlockSpec((DEC_SEQ, 512), lambda b, rb: (b, kt)),
            pl.BlockSpec((DEC_SEQ, 512), lambda b, rb: (b, vt)),
            pl.BlockSpec((None, None, PAST_LEN, 512), lambda b, rb: (b, l, 0, 0)),
            pl.BlockSpec((None, None, PAST_LEN, 512), lambda b, rb: (b, l, 0, 0)),
            pl.BlockSpec((NA_HEADS, 16, GRID_W, 128), lambda b, rb: (0, 0, 0, 0)),
        ],
        out_specs=pl.BlockSpec((512, 512), lambda b, rb: (b * 8 + rb, 0)),
        out_shape=jax.ShapeDtypeStruct((DEC_BATCH * DEC_SEQ, 512), BF16),
        compiler_params=_cparams(("arbitrary", "arbitrary"), 56),
        name="nbr_attn_lat",
    )(qa, ka, va, cache_k, cache_v, bias_tab)


def _na_ctx_kernel(q_ref, k_ref, v_ref, o_ref):
    ones_b = jnp.ones((q_ref.shape[0], 128), BF16)
    lane = lax.broadcasted_iota(I32, (q_ref.shape[0], 128), 1)
    for g in range(NA_HEADS // 2):
        sl = slice(g * 128, (g + 1) * 128)
        q2 = q_ref[:, sl]
        k_t = k_ref[:, sl].astype(F32).T.astype(BF16)
        va = jnp.concatenate([v_ref[:, sl], ones_b], axis=1)
        acc_a = _da_block(_half_masked(q2, False), k_t, va, None, None)[1]
        acc_b = _da_block(_half_masked(q2, True), k_t, va, None, None)[1]
        o_pair = jnp.where(lane < 64, acc_a[:, :128] / acc_a[:, 128:], acc_b[:, :128] / acc_b[:, 128:])
        o_ref[:, sl] = o_pair.astype(BF16)


def _na_ctx(cols):
    (qa, qt), (ka, kt), (va, vt) = cols["na_q"], cols["na_k"], cols["na_v"]
    return pl.pallas_call(
        _na_ctx_kernel,
        grid=(BATCH,),
        in_specs=[
            pl.BlockSpec((SEQ, 512), lambda b: (b, qt)),
            pl.BlockSpec((SEQ, 512), lambda b: (b, kt)),
            pl.BlockSpec((SEQ, 512), lambda b: (b, vt)),
        ],
        out_specs=pl.BlockSpec((SEQ, 512), lambda b: (b, 0)),
        out_shape=jax.ShapeDtypeStruct((BATCH * SEQ, 512), BF16),
        compiler_params=_cparams(("arbitrary",), 32),
        name="nbr_attn_ctx",
    )(qa, ka, va)


def _log_sigmoid(x):
    return jnp.minimum(x, 0.0) - jnp.log1p(jnp.exp(-jnp.abs(x)))


def _split3(x):
    x1 = x.astype(BF16)
    r1 = x - x1.astype(F32)
    x2 = r1.astype(BF16)
    x3 = (r1 - x2.astype(F32)).astype(BF16)
    return x1, x2, x3


def _ml_kernel(*refs, t_len, nb, unroll, zero_init, emit_state):
    q_ref, k_ref, v_ref, og_ref, g_ref = refs[:5]
    pos = 5
    if not zero_init:
        c0_ref, n0_ref, m0_ref = refs[pos:pos + 3]
        pos += 3
    nrm_ref = refs[pos]
    pos += 2 if emit_state else 1
    o_ref = refs[pos]
    pos += 1
    if emit_state:
        cst_ref, nst_ref, mst_ref = refs[pos:pos + 3]
        pos += 3
    hsum_scr, colb_scr, cmb_scr, rowt_scr, s_scr, m_scr = refs[pos:pos + 6]
    step = pl.program_id(0)
    lc = ML_LC
    nc = t_len // lc
    chains = [(bi, h, d) for bi in range(nb) for h in range(ML_HEADS) for d in range(2)]

    ri = lax.broadcasted_iota(I32, (lc, lc), 0)
    ci = lax.broadcasted_iota(I32, (lc, lc), 1)
    lo = ri >= ci
    up = ri <= ci
    lo_b = jnp.where(lo, 1.0, 0.0).astype(BF16)
    up_b = jnp.where(up, 1.0, 0.0).astype(BF16)
    lane = lax.broadcasted_iota(I32, (lc, 128), 1)
    trow = lax.broadcasted_iota(I32, (lc, 128), 0)
    is_forget = (lane & 1) == 1
    is_bwd = (lane & 2) == 2

    def tri_left(tri, x):
        return sum(_dot(tri, p) for p in _split3(x))

    def prep(c, carry):
        r0 = pl.multiple_of(c * lc, lc)
        g = g_ref[pl.ds(r0, lc), :]
        lf = _log_sigmoid(g)
        cb = jnp.where(is_forget, jnp.where(is_bwd, tri_left(up_b, lf), tri_left(lo_b, lf)), g)
        colb_scr[pl.ds(r0, lc), :] = cb
        ar = pltpu.roll(cb, 127, 1) - cb
        rowt_scr[c] = ar.T[0:16]
        pre = suf = -ar
        k = 1
        while k < lc:
            pre = jnp.maximum(pre, jnp.where(trow >= k, pltpu.roll(pre, k, 0), -jnp.inf))
            suf = jnp.maximum(suf, jnp.where(trow < lc - k, pltpu.roll(suf, lc - k, 0), -jnp.inf))
            k *= 2
        cmb_scr[pl.ds(r0, lc), :] = jnp.where(is_bwd, suf, pre)
        return carry

    lax.fori_loop(0, nb * nc, prep, 0, unroll=2)

    hsum_scr[...] = jnp.zeros_like(hsum_scr)
    if zero_init:
        s_scr[...] = jnp.zeros_like(s_scr)
        m_scr[...] = jnp.zeros_like(m_scr)
    else:
        for idx, (bi, h, d) in enumerate(chains):
            n_rep = jnp.broadcast_to(n0_ref[d, h:h + 1, :], (ML_DIM, ML_DIM)).T
            s_scr[idx * ML_DIM:(idx + 1) * ML_DIM, :] = jnp.concatenate([c0_ref[d, h], n_rep], axis=1)
            m0 = m0_ref[((step * nb + bi) * 2 + d) * ML_HEADS + h]
            m_scr[idx:idx + 1, :] = jnp.full((1, 128), m0, F32)

    ones_b = jnp.ones((lc, ML_DIM), BF16)
    n_ch = len(chains)

    def body(i, carry):
        cs = [i if d == 0 else nc - 1 - i for (_, _, d) in chains]
        r0s = [pl.multiple_of(bi * t_len + c * lc, lc) for (bi, _, _), c in zip(chains, cs)]
        hsl = [slice(h * 128, (h + 1) * 128) for (_, h, _) in chains]
        qs = [q_ref[pl.ds(r0, lc), sl] for r0, sl in zip(r0s, hsl)]
        ks = [k_ref[pl.ds(r0, lc), sl] for r0, sl in zip(r0s, hsl)]
        vs = [v_ref[pl.ds(r0, lc), sl] for r0, sl in zip(r0s, hsl)]
        s_old = [s_scr[idx * ML_DIM:(idx + 1) * ML_DIM, :] for idx in range(n_ch)]
        qk = [_dot_nt(q, k) for q, k in zip(qs, ks)]

        lhs, mts, bcs, ics, mms = [], [], [], [], []
        for idx, (bi, h, d) in enumerate(chains):
            l0 = h * 4 + d * 2
            cb = colb_scr[pl.ds(r0s[idx], lc), :]
            ic = cb[:, l0:l0 + 1]
            bc = cb[:, l0 + 1:l0 + 2]
            arow = rowt_scr[bi * nc + cs[idx], l0:l0 + 1, :]
            mm = m_scr[idx:idx + 1, 0:1]
            cm = cmb_scr[pl.ds(r0s[idx], lc), :][:, l0:l0 + 1]
            gmax = jnp.maximum(mm, cm)
            mt = bc + gmax
            sc = qk[idx] * jnp.exp(jnp.where(lo if d == 0 else up, -(gmax + arow), -jnp.inf))
            inter = jnp.exp(mm - gmax)
            lhs.append(jnp.concatenate([(inter * qs[idx].astype(F32)).astype(BF16), sc.astype(BF16)], axis=1))
            mts.append(mt)
            bcs.append(bc)
            ics.append(ic)
            mms.append(mm)

        nds = [_dot(a, jnp.concatenate([so.astype(BF16), jnp.concatenate([v, ones_b], axis=1)], axis=0))
               for a, so, v in zip(lhs, s_old, vs)]

        hcs, wvs, decays, m_news = [], [], [], []
        for idx, (bi, h, d) in enumerate(chains):
            last = lc - 1 if d == 0 else 0
            nd = nds[idx]
            hcs.append(nd[:, :ML_DIM] / jnp.maximum(jnp.abs(nd[:, ML_DIM:]), jnp.exp(-mts[idx])))
            m_new = mts[idx][last:last + 1, :]
            bl = bcs[idx][last:last + 1, :]
            w = jnp.exp(bl - bcs[idx] + ics[idx] - m_new)
            decays.append(jnp.exp(bl + mms[idx] - m_new))
            m_news.append(jnp.broadcast_to(m_new, (1, 128)))
            wvs.append(jnp.concatenate([(w * vs[idx].astype(F32)).astype(BF16),
                                        jnp.broadcast_to(w, (lc, ML_DIM)).astype(BF16)], axis=1))
        kts = [k.astype(F32).T.astype(BF16) for k in ks]
        kv = [_dot(kt, wv) for kt, wv in zip(kts, wvs)]
        s_scr[...] = jnp.concatenate([dc * so + x for dc, so, x in zip(decays, s_old, kv)], axis=0)
        m_scr[...] = jnp.concatenate(m_news, axis=0)
        for bi in range(nb):
            for d in range(2):
                sel = [idx for idx, ch in enumerate(chains) if ch[0] == bi and ch[2] == d]
                hsum_scr[pl.ds(r0s[sel[0]], lc), :] += jnp.concatenate([hcs[idx] for idx in sel], axis=1)
        return carry

    lax.fori_loop(0, nc, body, 0, unroll=unroll)

    for bi in range(nb):
        rows = slice(bi * t_len, (bi + 1) * t_len)
        for h in range(ML_HEADS):
            sl = slice(h * 128, (h + 1) * 128)
            hs = hsum_scr[rows, sl]
            ms = jnp.mean(hs * hs, axis=-1, keepdims=True)
            y = hs * lax.rsqrt(ms + EPS) * nrm_ref[:, sl]
            o_ref[rows, sl] = (y * jax.nn.sigmoid(og_ref[rows, sl].astype(F32))).astype(BF16)
    if emit_state:
        for idx, (bi, h, d) in enumerate(chains):
            st = s_scr[idx * ML_DIM:(idx + 1) * ML_DIM, :]
            cst_ref[bi, d, h] = st[:, :ML_DIM]
            nst_ref[bi, d, h:h + 1, :] = st[:, ML_DIM:].T[0:1, :]
            mst_ref[bi, d, h:h + 1, :] = m_scr[idx:idx + 1, :]


def _mlstm(cols, gates, states, c_out, l, ml_norm_l, *, prompt):
    names = ("ml_q", "ml_k", "ml_v", "ml_o")
    m = cols["ml_q"][0].shape[0]
    t_len = SEQ if prompt else DEC_SEQ
    nb = 2 if prompt else 1
    n_seq = m // t_len
    rows = nb * t_len
    big = {} if prompt else dict(pipeline_mode=pl.Buffered(1))
    in_specs = [pl.BlockSpec((rows, 512), functools.partial(lambda i, t: (i, t), t=cols[n][1]), **big)
                for n in names]
    in_specs.append(pl.BlockSpec((rows, 128), lambda i: (i, 0)))
    args = [cols[n][0] for n in names] + [gates]
    if not prompt:
        c0, n0, m0 = states
        in_specs += [
            pl.BlockSpec((None, None, 2, ML_HEADS, ML_DIM, ML_DIM), lambda i: (i, l, 0, 0, 0, 0)),
            pl.BlockSpec((None, None, 2, ML_HEADS, ML_DIM), lambda i: (i, l, 0, 0, 0)),
            pl.BlockSpec(memory_space=pltpu.SMEM),
        ]
        args += [c0, n0, m0[:, l].reshape(-1)]
    in_specs.append(pl.BlockSpec((1, 512), lambda i: (0, 0)))
    args.append(ml_norm_l.reshape(1, 512))
    out_shape = [jax.ShapeDtypeStruct((m, 512), BF16)]
    out_specs = [pl.BlockSpec((rows, 512), lambda i: (i, 0))]
    aliases = {}
    if prompt:
        aliases = {len(args): 1}
        in_specs.append(pl.BlockSpec(memory_space=pl.ANY))
        args.append(c_out)
        out_shape += [
            jax.ShapeDtypeStruct(c_out.shape, F32),
            jax.ShapeDtypeStruct((n_seq, 2, ML_HEADS, ML_DIM), F32),
            jax.ShapeDtypeStruct((n_seq, 2, ML_HEADS, 128), F32),
        ]
        out_specs += [
            pl.BlockSpec((nb, None, 2, ML_HEADS, ML_DIM, ML_DIM), lambda i: (i, l, 0, 0, 0, 0)),
            pl.BlockSpec((nb, 2, ML_HEADS, ML_DIM), lambda i: (i, 0, 0, 0)),
            pl.BlockSpec((nb, 2, ML_HEADS, 128), lambda i: (i, 0, 0, 0)),
        ]
    n_chain = nb * ML_HEADS * 2
    return pl.pallas_call(
        functools.partial(_ml_kernel, t_len=t_len, nb=nb, unroll=2,
                          zero_init=prompt, emit_state=prompt),
        grid=(n_seq // nb,),
        in_specs=in_specs,
        out_specs=out_specs,
        out_shape=out_shape,
        input_output_aliases=aliases,
        scratch_shapes=[pltpu.VMEM((rows, 512), F32), pltpu.VMEM((rows, 128), F32), pltpu.VMEM((rows, 128), F32),
                        pltpu.VMEM((rows // ML_LC, 16, ML_LC), F32),
                        pltpu.VMEM((n_chain * ML_DIM, 2 * ML_DIM), F32), pltpu.VMEM((n_chain, 128), F32)],
        compiler_params=_cparams(("arbitrary",), 56),
        name="mlstm_ctx" if prompt else "mlstm_lat",
    )(*args)


def _merge_kernel(x_ref, mod_ref, oda_ref, oml_ref, ona_ref, g0_ref, g1_ref, g2_ref,
                  wda_ref, wml_ref, wna_ref, wo_ref, o_ref):
    def branch(o, w, g):
        return jax.nn.sigmoid(g[...].astype(F32)) * _dot(o[...], w[...])

    merged = (branch(oda_ref, wda_ref, g0_ref) + branch(oml_ref, wml_ref, g1_ref)
              + branch(ona_ref, wna_ref, g2_ref))
    o_ref[...] = x_ref[...] + mod_ref[2:3, :] * _dot(merged.astype(BF16), wo_ref[...])


def _merge_out(x, modl, o_da, o_ml, o_na, cols, w_da, w_ml, w_na, w_out, *, prompt):
    m = x.shape[0]
    tm = 512
    const = lambda i: (0, 0)
    a, gt = cols["merge"]
    g0 = gt // 2
    return pl.pallas_call(
        _merge_kernel,
        grid=(m // tm,),
        in_specs=[
            pl.BlockSpec((tm, D_MODEL), lambda i: (i, 0)),
            pl.BlockSpec((None, 6, D_MODEL), _mod_row_map(prompt, tm)),
            pl.BlockSpec((tm, 512), lambda i: (i, 0)),
            pl.BlockSpec((tm, 512), lambda i: (i, 0)),
            pl.BlockSpec((tm, 512), lambda i: (i, 0)),
            pl.BlockSpec((tm, D_MODEL), lambda i: (i, g0)),
            pl.BlockSpec((tm, D_MODEL), lambda i: (i, g0 + 1)),
            pl.BlockSpec((tm, D_MODEL), lambda i: (i, g0 + 2)),
            pl.BlockSpec((512, D_MODEL), const),
            pl.BlockSpec((512, D_MODEL), const),
            pl.BlockSpec((512, D_MODEL), const),
            pl.BlockSpec((D_MODEL, D_MODEL), const),
        ],
        out_specs=pl.BlockSpec((tm, D_MODEL), lambda i: (i, 0)),
        out_shape=jax.ShapeDtypeStruct((m, D_MODEL), F32),
        compiler_params=_cparams(("arbitrary",), 48),
        name="merge_out_ctx" if prompt else "merge_out_lat",
    )(x, modl, o_da, o_ml, o_na, a, a, a, w_da, w_ml, w_na, w_out)


def _ffn_kernel(x_ref, mod_ref, g_ref, w1_ref, b1_ref, w2_ref, b2_ref, *refs, final):
    o_ref, h_scr, acc_scr = refs[-3:]
    f = pl.program_id(1)

    @pl.when(f == 0)
    def _():
        h_scr[...] = _modulated_norm(x_ref[...], g_ref[...], mod_ref[3:4, :], mod_ref[4:5, :]).astype(BF16)
        acc_scr[...] = jnp.zeros_like(acc_scr)

    h = h_scr[...]
    part = None
    for c in range(w1_ref.shape[1] // IP_SUB):
        cs = slice(c * IP_SUB, (c + 1) * IP_SUB)
        u = jnp.maximum(_dot(h, w1_ref[:, cs]) + b1_ref[:, cs], 0.0)
        d = _dot((u * u).astype(BF16), w2_ref[cs, :])
        part = d if part is None else part + d
    acc_scr[...] += part

    @pl.when(f == pl.num_programs(1) - 1)
    def _():
        y = x_ref[...] + mod_ref[5:6, :] * (acc_scr[...] + b2_ref[...])
        if final:
            ms = jnp.mean(y * y, axis=-1, keepdims=True)
            y = y * lax.rsqrt(ms + EPS) * refs[0][...]
        o_ref[...] = y


def _ffn(x, modl, g2, w1, b1, w2, b2, g_final, l, *, prompt):
    m = x.shape[0]
    tm, tf = 1024, 1024
    final = g_final is not None
    in_specs = [
        pl.BlockSpec((tm, D_MODEL), lambda i, f: (i, 0)),
        pl.BlockSpec((None, 6, D_MODEL), _mod_row_map(prompt, tm)),
        pl.BlockSpec((1, D_MODEL), lambda i, f: (0, 0)),
        pl.BlockSpec((None, D_MODEL, tf), lambda i, f: (l, 0, f)),
        pl.BlockSpec((1, tf), lambda i, f: (0, f)),
        pl.BlockSpec((None, tf, D_MODEL), lambda i, f: (l, f, 0)),
        pl.BlockSpec((1, D_MODEL), lambda i, f: (0, 0)),
    ]
    args = [x, modl, g2, w1, b1, w2, b2]
    if final:
        in_specs.append(pl.BlockSpec((1, D_MODEL), lambda i, f: (0, 0)))
        args.append(g_final)
    return pl.pallas_call(
        functools.partial(_ffn_kernel, final=final),
        grid=(m // tm, D_FF // tf),
        in_specs=in_specs,
        out_specs=pl.BlockSpec((tm, D_MODEL), lambda i, f: (i, 0)),
        out_shape=jax.ShapeDtypeStruct((m, D_MODEL), F32),
        scratch_shapes=[pltpu.VMEM((tm, D_MODEL), BF16), pltpu.VMEM((tm, D_MODEL), F32)],
        compiler_params=_cparams(("arbitrary", "arbitrary"), 48),
        name="ffn_ctx" if prompt else "ffn_lat",
    )(*args)


def _rope_tables():
    t = jnp.arange(DEC_SEQ)
    lane = jnp.arange(128)
    sub = lane % 32
    freq = ROPE_BASE ** (-(2 * (sub % 16)).astype(F32) / 32.0)
    use_row = (lane % 64) < 32
    posv = jnp.where(use_row[None, :], (t // GRID_W)[:, None], (t % GRID_W)[:, None]).astype(F32)
    ang = posv * freq[None, :]
    sign = jnp.where(sub < 16, -1.0, 1.0).astype(F32)
    return jnp.cos(ang), jnp.sin(ang) * sign[None, :]


def kernel(x_prompt, x_sample, cache_da_k, cache_da_v, cache_na_k, cache_na_v, state_ml_C, state_ml_n,
           state_ml_m, c, c_ctx, w_mod, b_mod, norm1, w_in, b_in, da_lam, da_subln, ml_norm, na_rpb,
           w_up_da, w_up_ml, w_up_na, w_out, norm2, w_ff1, b_ff1, w_ff2, b_ff2, norm_f):
    xp = x_prompt.reshape(BATCH * SEQ, D_MODEL)
    xs = x_sample.reshape(DEC_BATCH * DEC_SEQ, D_MODEL)

    w_main, w_g = _stage_w_in(jnp.swapaxes(w_in, 1, 2))
    b_main = jnp.concatenate([b_in[..., a:b] for a, b in W_SEGS], axis=-1).reshape(DEPTH, 1, N_MAIN)
    tile_of = {n: t for t, n in enumerate(W_TILES)}
    src, dst = [], []
    for hh in range(ML_HEADS):
        for dr in range(2):
            for tt in range(2):
                src.append(GATE_OFF + dr * 2 * ML_HEADS + tt * ML_HEADS + hh)
                dst.append(hh * 4 + dr * 2 + tt)
    src, dst = np.array(src), np.array(dst)
    b_g = jnp.zeros((DEPTH, 1, 128), F32).at[:, 0, dst].set(b_in[:, src])
    cs = np.ones((1, N_MAIN), np.float32)
    for name, scale in (("da_q", DA_QK ** -0.5), ("ml_k", ML_DIM ** -0.5), ("na_q", NA_DIM ** -0.5)):
        cs[0, tile_of[name] * 512:(tile_of[name] + 1) * 512] = scale
    cscale = jnp.asarray(cs)
    w_da_b, w_ml_b, w_na_b, w_out_b = (w.astype(BF16) for w in (w_up_da, w_up_ml, w_up_na, w_out))
    w_ff1_b, w_ff2_b = w_ff1.astype(BF16), w_ff2.astype(BF16)
    rope_tabs = _rope_tables()

    cc = jnp.zeros((8, D_MODEL), F32).at[0].set(c_ctx).at[1:1 + DEC_BATCH].set(c)
    mod = _modulation(cc, w_mod, b_mod).reshape(DEPTH, 8, 6, D_MODEL)

    cdk = cache_da_k.reshape(DEC_BATCH, DEPTH, PAST_LEN, 512)
    cdv = cache_da_v.reshape(DEC_BATCH, DEPTH, PAST_LEN, 512)
    cnk = cache_na_k.reshape(DEC_BATCH, DEPTH, PAST_LEN, 512)
    cnv = cache_na_v.reshape(DEC_BATCH, DEPTH, PAST_LEN, 512)

    kv_outs = [jnp.zeros((BATCH, DEPTH, SEQ, DA_HEADS, 2 * DA_QK), F32) for _ in range(2)]
    kv_outs += [jnp.zeros((BATCH, DEPTH, NA_HEADS, NA_DIM, SEQ), F32) for _ in range(2)]
    c_out = jnp.zeros((BATCH, DEPTH, 2, ML_HEADS, ML_DIM, ML_DIM), F32)
    coll_n, coll_m = [], []
    gf = norm_f.reshape(1, D_MODEL)
    for l in range(DEPTH):
        lam_init = 0.8 - 0.6 * math.exp(-0.3 * l)
        modl = mod[l]
        g1 = norm1[l].reshape(1, D_MODEL)
        g2 = norm2[l].reshape(1, D_MODEL)
        sub_row = da_subln[l].reshape(1, 128)
        bias_tab = _na_bias_table(na_rpb[l])
        for prompt in (True, False):
            x = xp if prompt else xs
            if prompt:
                a_kv, kv_outs = _in_proj_kv(x, modl, g1, w_main, b_main[l], kv_outs, l)
                a, gates = _in_proj(x, modl, g1, w_main, b_main[l], cscale, w_g, b_g[l], l,
                                    col0=N_KV, prompt=True)
                cols = {n: (a_kv, t) if t < 4 else (a, t - 4) for n, t in tile_of.items()}
            else:
                a, gates = _in_proj(x, modl, g1, w_main, b_main[l], cscale, w_g, b_g[l], l,
                                    col0=0, prompt=False)
                cols = {n: (a, t) for n, t in tile_of.items()}
            o_da = _diff_attention(cols, cdk, cdv, rope_tabs, l, da_lam[l], sub_row, lam_init, prompt=prompt)
            ml_out = _mlstm(cols, gates, (state_ml_C, state_ml_n, state_ml_m), c_out, l, ml_norm[l],
                            prompt=prompt)
            o_ml = ml_out[0]
            if prompt:
                c_out = ml_out[1]
                coll_n.append(ml_out[2])
                coll_m.append(ml_out[3][..., 0])
                o_na = _na_ctx(cols)
            else:
                o_na = _na_lat(cols, cnk, cnv, l, bias_tab)
            x = _merge_out(x, modl, o_da, o_ml, o_na, cols, w_da_b[l], w_ml_b[l], w_na_b[l], w_out_b[l],
                           prompt=prompt)
            x = _ffn(x, modl, g2, w_ff1_b, b_ff1[l].reshape(1, D_FF), w_ff2_b,
                     b_ff2[l].reshape(1, D_MODEL), gf if l == DEPTH - 1 else None, l, prompt=prompt)
            if prompt:
                xp = x
            else:
                xs = x

    y_prompt = xp.reshape(BATCH, SEQ, D_MODEL)
    y_sample = xs.reshape(DEC_BATCH, DEC_SEQ, D_MODEL)
    new_da_k, new_da_v = kv_outs[0], kv_outs[1]
    new_na_k = jnp.transpose(kv_outs[2], (0, 1, 4, 2, 3))
    new_na_v = jnp.transpose(kv_outs[3], (0, 1, 4, 2, 3))
    new_ml_n = jnp.stack(coll_n, axis=1)
    new_ml_m = jnp.stack(coll_m, axis=1)
    return (y_prompt, y_sample, new_da_k, new_da_v, new_na_k, new_na_v, c_out, new_ml_n, new_ml_m)
```
